```python
import math
import jax, jax.numpy as jnp
from jax import lax
import numpy as np

D_MODEL = 4096
BATCH = 2
SEQ = 8192
DEPTH = 4

CHUNK = 64
Q_BLOCK = 128
N_A_LAYERS = DEPTH // 2
N_B_LAYERS = DEPTH - N_A_LAYERS
MIX_WIDTH = 3 * D_MODEL // 4
MEM_HEADS = 4
MEM_HEAD_DIM = (D_MODEL - MIX_WIDTH) // MEM_HEADS
MEM_WIDTH = MEM_HEADS * MEM_HEAD_DIM
MEM_TOKENS = 256
RET_HEADS = 12
RET_HEAD_DIM = MIX_WIDTH // RET_HEADS
DIFF_HEADS = 12
DIFF_HEAD_DIM = MIX_WIDTH // (2 * DIFF_HEADS)
A_IN_WIDTH = 4 * MIX_WIDTH + MEM_WIDTH
B_IN_WIDTH = MIX_WIDTH + MEM_WIDTH
KV_WIDTH = 2 * MIX_WIDTH
D_FF = 2 * D_MODEL
CONV_WIDTH = 3
NUM_BUCKETS = 32
MAX_DISTANCE = 128
ROPE_BASE = 10000.0
EPS = 1e-6

kernel_name = "yoco_retention_diffattn_conv_mlp_trunk"


def rms_norm(x, g):
    xf = x.astype(jnp.float32)
    y = xf * lax.rsqrt(jnp.mean(xf * xf, axis=-1, keepdims=True) + EPS)
    return (y * g.astype(jnp.float32)).astype(x.dtype)


def head_group_norm(o):
    mu = jnp.mean(o, axis=-1, keepdims=True)
    oc = o - mu
    return oc * lax.rsqrt(jnp.mean(oc * oc, axis=-1, keepdims=True) + EPS)


def rotary(t, pos):
    half = t.shape[-1] // 2
    inv_freq = ROPE_BASE ** (-jnp.arange(half, dtype=jnp.float32) / half)
    ang = pos.astype(jnp.float32)[:, None] * inv_freq[None, :]
    cos = jnp.cos(ang)[None, :, None, :]
    sin = jnp.sin(ang)[None, :, None, :]
    tf = t.astype(jnp.float32)
    t1, t2 = tf[..., :half], tf[..., half:]
    return jnp.concatenate([t1 * cos - t2 * sin, t2 * cos + t1 * sin], axis=-1)


def retention(q, k, v):
    b, s, h, dk = q.shape
    dv = v.shape[-1]
    nc = s // CHUNK
    log_gamma = jnp.log(1.0 - 2.0 ** (-5.0 - jnp.arange(h, dtype=jnp.float32)))
    qc = q.reshape(b, nc, CHUNK, h, dk)
    kc = k.reshape(b, nc, CHUNK, h, dk)
    vc = v.reshape(b, nc, CHUNK, h, dv)
    idx = jnp.arange(CHUNK, dtype=jnp.float32)
    dist = jnp.abs(idx[:, None] - idx[None, :])
    decay_intra = jnp.exp(log_gamma[:, None, None] * dist)
    scores = jnp.einsum('bnihd,bnjhd->bnhij', qc, kc) * decay_intra
    o_intra = jnp.einsum('bnhij,bnjhe->bnihe', scores, vc)
    q_decay = jnp.exp((idx[:, None] + 1.0) * log_gamma[None, :])[..., None]
    k_decay = jnp.exp((CHUNK - 1.0 - idx[:, None]) * log_gamma[None, :])[..., None]
    chunk_decay = jnp.exp(CHUNK * log_gamma)[None, :, None, None]

    def step(state, inp):
        qn, kn, vn = inp
        o = jnp.einsum('bihd,bhde->bihe', qn * q_decay, state)
        state = state * chunk_decay + jnp.einsum('bjhd,bjhe->bhde', kn * k_decay, vn)
        return state, o

    xs = (jnp.moveaxis(qc, 1, 0), jnp.moveaxis(kc, 1, 0), jnp.moveaxis(vc, 1, 0))
    state0 = jnp.zeros((b, h, dk, dv), jnp.float32)
    _, o_inter = lax.scan(step, state0, xs)
    o = o_intra + jnp.moveaxis(o_inter, 0, 1)
    return o.reshape(b, s, h, dv)


def t5_bucket(rel):
    half = NUM_BUCKETS // 2
    max_exact = half // 2
    ret = jnp.where(rel > 0, half, 0)
    n = jnp.abs(rel)
    nf = jnp.maximum(n, 1).astype(jnp.float32)
    large = max_exact + (jnp.log(nf / max_exact) / math.log(MAX_DISTANCE / max_exact)
                         * (half - max_exact)).astype(jnp.int32)
    large = jnp.minimum(large, half - 1)
    return ret + jnp.where(n < max_exact, n, large)


def diff_attention(q, k, v, lam, lam_init, bias_table, subln_g):
    b, s, h, _, dh = q.shape
    nb = s // Q_BLOCK
    scale = dh ** -0.5
    kpos = jnp.arange(s)
    kchunk = kpos // CHUNK
    qb = jnp.moveaxis(q.reshape(b, nb, Q_BLOCK, h, 2, dh), 1, 0)
    table = bias_table.astype(jnp.float32)

    def block(args):
        qblk, bi = args
        qpos = bi * Q_BLOCK + jnp.arange(Q_BLOCK)
        logits = jnp.einsum('bqhtd,bkhtd->bhtqk', qblk, k).astype(jnp.float32) * scale
        bias = jnp.transpose(table[t5_bucket(kpos[None, :] - qpos[:, None])], (2, 0, 1))
        allowed = kchunk[None, :] <= (qpos // CHUNK)[:, None]
        logits = jnp.where(allowed, logits + bias[None, :, None], -jnp.inf)
        p = jax.nn.softmax(logits, axis=-1)
        w = p[:, :, 0] - lam * p[:, :, 1]
        return jnp.einsum('bhqk,bkhe->bqhe', w.astype(v.dtype), v)

    o = lax.map(block, (qb, jnp.arange(nb)))
    o = jnp.moveaxis(o, 0, 1).reshape(b, s, h, 2 * dh)
    return rms_norm(o, subln_g) * (1.0 - lam_init)


def memory_attention(q, mk, mv):
    scale = q.shape[-1] ** -0.5
    logits = jnp.einsum('bshd,bmhd->bhsm', q, mk).astype(jnp.float32) * scale
    p = jax.nn.softmax(logits, axis=-1).astype(mv.dtype)
    return jnp.einsum('bhsm,bmhd->bshd', p, mv)


def conv_gated_mlp(h, w_up, conv_w, conv_b, w_down):
    u = h @ w_up
    s = u.shape[1]
    up = jnp.pad(u, ((0, 0), (CONV_WIDTH - 1, 0), (0, 0)))
    y = conv_b
    for tap in range(CONV_WIDTH):
        y = y + conv_w[tap] * up[:, tap:tap + s]
    a, g = jnp.split(y, 2, axis=-1)
    return (jax.nn.silu(a) * g) @ w_down


def setup_inputs(seed: int = 0) -> dict:
    key = jax.random.key(seed)
    ks = jax.random.split(key, 24)
    f32 = jnp.float32

    def w(k, shape, fan_in):
        return jax.random.normal(k, shape, f32) * fan_in ** -0.5

    def gain(k, shape):
        return 1.0 + 0.01 * jax.random.normal(k, shape, f32)

    return {
        'x': jax.random.normal(ks[0], (BATCH, SEQ, D_MODEL), f32),
        'mem': jax.random.normal(ks[1], (BATCH, MEM_TOKENS, D_MODEL), f32),
        'a_w_in': w(ks[2], (N_A_LAYERS, D_MODEL, A_IN_WIDTH), D_MODEL),
        'b_w_q': w(ks[3], (N_B_LAYERS, D_MODEL, B_IN_WIDTH), D_MODEL),
        'w_kv': w(ks[4], (D_MODEL, KV_WIDTH), D_MODEL),
        'w_out': w(ks[5], (DEPTH, D_MODEL, D_MODEL), D_MODEL),
        'mem_w_kv': w(ks[6], (DEPTH, D_MODEL, 2 * MEM_WIDTH), D_MODEL),
        'ffn_w_up': w(ks[7], (DEPTH, D_MODEL, 2 * D_FF), D_MODEL),
        'ffn_conv_w': w(ks[8], (DEPTH, CONV_WIDTH, 2 * D_FF), CONV_WIDTH),
        'ffn_conv_b': 0.01 * jax.random.normal(ks[9], (DEPTH, 2 * D_FF), f32),
        'ffn_w_down': w(ks[10], (DEPTH, D_FF, D_MODEL), D_FF),
        'attn_norm': gain(ks[11], (DEPTH, D_MODEL)),
        'ffn_norm': gain(ks[12], (DEPTH, D_MODEL)),
        'kv_norm': gain(ks[13], (D_MODEL,)),
        'mem_norm': gain(ks[14], (D_MODEL,)),
        'final_norm': gain(ks[15], (D_MODEL,)),
        'rel_bias': 0.5 * jax.random.normal(ks[16], (NUM_BUCKETS, DIFF_HEADS), f32),
        'lambda_q1': 0.1 * jax.random.normal(ks[17], (N_B_LAYERS, DIFF_HEAD_DIM), f32),
        'lambda_k1': 0.1 * jax.random.normal(ks[18], (N_B_LAYERS, DIFF_HEAD_DIM), f32),
        'lambda_q2': 0.1 * jax.random.normal(ks[19], (N_B_LAYERS, DIFF_HEAD_DIM), f32),
        'lambda_k2': 0.1 * jax.random.normal(ks[20], (N_B_LAYERS, DIFF_HEAD_DIM), f32),
        'diff_subln': gain(ks[21], (N_B_LAYERS, 2 * DIFF_HEAD_DIM)),
    }


def reference(x, mem, a_w_in, b_w_q, w_kv, w_out, mem_w_kv, ffn_w_up, ffn_conv_w, ffn_conv_b,
              ffn_w_down, attn_norm, ffn_norm, kv_norm, mem_norm, final_norm, rel_bias,
              lambda_q1, lambda_k1, lambda_q2, lambda_k2, diff_subln):
    b, s, _ = x.shape
    m = mem.shape[1]
    pos = jnp.arange(s)
    memn = rms_norm(mem, mem_norm)
    shared_k = None
    shared_v = None
    for layer in range(DEPTH):
        if layer == N_A_LAYERS:
            kv = rms_norm(x, kv_norm) @ w_kv
            sk, sv = jnp.split(kv, [MIX_WIDTH], axis=-1)
            shared_k = sk.reshape(b, s, DIFF_HEADS, 2, DIFF_HEAD_DIM)
            shared_v = sv.reshape(b, s, DIFF_HEADS, 2 * DIFF_HEAD_DIM)
        h = rms_norm(x, attn_norm[layer])
        mk, mv = jnp.split(memn @ mem_w_kv[layer], 2, axis=-1)
        mk = mk.reshape(b, m, MEM_HEADS, MEM_HEAD_DIM)
        mv = mv.reshape(b, m, MEM_HEADS, MEM_HEAD_DIM)
        if layer < N_A_LAYERS:
            proj = h @ a_w_in[layer]
            q, k, v, g, mq = jnp.split(proj, [MIX_WIDTH, 2 * MIX_WIDTH, 3 * MIX_WIDTH, 4 * MIX_WIDTH], axis=-1)
            q = rotary(q.reshape(b, s, RET_HEADS, RET_HEAD_DIM), pos)
            k = rotary(k.reshape(b, s, RET_HEADS, RET_HEAD_DIM), pos) * (RET_HEAD_DIM ** -0.5)
            v = v.reshape(b, s, RET_HEADS, RET_HEAD_DIM).astype(jnp.float32)
            o = head_group_norm(retention(q, k, v)).reshape(b, s, MIX_WIDTH)
            mix = (jax.nn.silu(g.astype(jnp.float32)) * o).astype(x.dtype)
        else:
            bl = layer - N_A_LAYERS
            proj = h @ b_w_q[bl]
            q, mq = jnp.split(proj, [MIX_WIDTH], axis=-1)
            q = q.reshape(b, s, DIFF_HEADS, 2, DIFF_HEAD_DIM)
            lam_init = 0.8 - 0.6 * math.exp(-0.3 * layer)
            lam = (jnp.exp(jnp.sum(lambda_q1[bl].astype(jnp.float32) * lambda_k1[bl].astype(jnp.float32)))
                   - jnp.exp(jnp.sum(lambda_q2[bl].astype(jnp.float32) * lambda_k2[bl].astype(jnp.float32)))
                   + lam_init)
            mix = diff_attention(q, shared_k, shared_v, lam, lam_init, rel_bias,
                                 diff_subln[bl]).reshape(b, s, MIX_WIDTH).astype(x.dtype)
        mo = memory_attention(mq.reshape(b, s, MEM_HEADS, MEM_HEAD_DIM), mk, mv).reshape(b, s, MEM_WIDTH)
        x = x + jnp.concatenate([mix, mo.astype(x.dtype)], axis=-1) @ w_out[layer]
        h = rms_norm(x, ffn_norm[layer])
        x = x + conv_gated_mlp(h, ffn_w_up[layer], ffn_conv_w[layer], ffn_conv_b[layer], ffn_w_down[layer])
    return rms_norm(x, final_norm)
```

```python
import functools
import math

import jax
import jax.numpy as jnp
from jax import lax
from jax.experimental import pallas as pl
from jax.experimental.pallas import tpu as pltpu

F32 = jnp.float32
BF16 = jnp.bfloat16

EPS = 1e-6
CHUNK = 64
ROPE_BASE = 10000.0
NUM_BUCKETS = 32
MAX_DISTANCE = 128
MEM_HEADS = 4
RET_HEADS = 12
DIFF_HEADS = 12
HEAD_W = 256
HALF_W = HEAD_W // 2

VMEM_LIMIT = 56 * 1024 * 1024


def _params(sem):
    return pltpu.CompilerParams(dimension_semantics=sem, vmem_limit_bytes=VMEM_LIMIT)


def _pick(n, pref):
    b = min(pref, n)
    while n % b:
        b //= 2
    return b


def _rmsnorm_kernel(x_ref, g_ref, o_ref):
    x = x_ref[...]
    ms = jnp.mean(x * x, axis=-1, keepdims=True)
    o_ref[...] = (x * lax.rsqrt(ms + EPS) * g_ref[...]).astype(o_ref.dtype)


def rmsnorm(x, g, out_dtype, bm=256):
    m, d = x.shape
    bm = _pick(m, bm)
    return pl.pallas_call(
        _rmsnorm_kernel,
        grid=(m // bm,),
        in_specs=[pl.BlockSpec((bm, d), lambda i: (i, 0)),
                  pl.BlockSpec((1, d), lambda i: (0, 0))],
        out_specs=pl.BlockSpec((bm, d), lambda i: (i, 0)),
        out_shape=jax.ShapeDtypeStruct((m, d), out_dtype),
        compiler_params=_params(("parallel",)),
        name="rmsnorm",
    )(x, g.reshape(1, d))


def _mm_kernel(a_ref, w_ref, o_ref):
    o_ref[...] = jnp.dot(a_ref[...], w_ref[...], preferred_element_type=F32).astype(o_ref.dtype)


def _mm_res_kernel(a_ref, w_ref, r_ref, o_ref):
    acc = jnp.dot(a_ref[...], w_ref[...], preferred_element_type=F32)
    o_ref[...] = (r_ref[...] + acc).astype(o_ref.dtype)


def _mm2_res_kernel(a1_ref, a2_ref, w1_ref, w2_ref, r_ref, o_ref):
    acc = jnp.dot(a1_ref[...], w1_ref[...], preferred_element_type=F32)
    acc = acc + jnp.dot(a2_ref[...], w2_ref[...], preferred_element_type=F32)
    o_ref[...] = (r_ref[...] + acc).astype(o_ref.dtype)


def matmul(a, w, out_dtype, resid=None, bm=1024, bn=1024):
    m, k = a.shape
    n = w.shape[1]
    bm, bn = _pick(m, bm), _pick(n, bn)
    in_specs = [pl.BlockSpec((bm, k), lambda i, j: (i, 0)),
                pl.BlockSpec((k, bn), lambda i, j: (0, j))]
    args = [a, w]
    body = _mm_kernel
    if resid is not None:
        in_specs.append(pl.BlockSpec((bm, bn), lambda i, j: (i, j)))
        args.append(resid)
        body = _mm_res_kernel
    return pl.pallas_call(
        body,
        grid=(m // bm, n // bn),
        in_specs=in_specs,
        out_specs=pl.BlockSpec((bm, bn), lambda i, j: (i, j)),
        out_shape=jax.ShapeDtypeStruct((m, n), out_dtype),
        compiler_params=_params(("parallel", "arbitrary")),
        name="matmul",
    )(*args)


def out_proj(mix, mo, w, resid, bm=512, bn=1024):
    m, k1 = mix.shape
    k2 = mo.shape[1]
    n = w.shape[1]
    assert k1 % k2 == 0
    bm, bn = _pick(m, bm), _pick(n, bn)
    return pl.pallas_call(
        _mm2_res_kernel,
        grid=(m // bm, n // bn),
        in_specs=[pl.BlockSpec((bm, k1), lambda i, j: (i, 0)),
                  pl.BlockSpec((bm, k2), lambda i, j: (i, 0)),
                  pl.BlockSpec((k1, bn), lambda i, j: (0, j)),
                  pl.BlockSpec((k2, bn), lambda i, j: (k1 // k2, j)),
                  pl.BlockSpec((bm, bn), lambda i, j: (i, j))],
        out_specs=pl.BlockSpec((bm, bn), lambda i, j: (i, j)),
        out_shape=jax.ShapeDtypeStruct((m, n), F32),
        compiler_params=_params(("parallel", "arbitrary")),
        name="out_proj",
    )(mix, mo, w, w, resid)


def _retention_kernel(q_ref, k_ref, v_ref, g_ref, cos_ref, sin_ref, dmat_ref, qd_ref, kd_ref, cd_ref,
                      o_ref, state_ref):
    @pl.when(pl.program_id(2) == 0)
    def _():
        state_ref[...] = jnp.zeros_like(state_ref)

    cos = cos_ref[...]
    sin = sin_ref[...]

    def rot(x):
        x1, x2 = x[:, :HALF_W], x[:, HALF_W:]
        return x1 * cos - x2 * sin, x2 * cos + x1 * sin

    q1, q2 = rot(q_ref[0].astype(F32))
    k1, k2 = rot(k_ref[0].astype(F32))
    kscale = HEAD_W ** -0.5
    k1, k2 = k1 * kscale, k2 * kscale
    v = v_ref[0]

    qd = qd_ref[0]
    kd = kd_ref[0]
    qr = jnp.concatenate([q1, q2], axis=-1).astype(BF16)
    kr = jnp.concatenate([k1, k2], axis=-1).astype(BF16)
    qdec = jnp.concatenate([q1 * qd, q2 * qd], axis=-1).astype(BF16)
    kdec = jnp.concatenate([k1 * kd, k2 * kd], axis=-1)

    scores = lax.dot_general(qr, kr, (((1,), (1,)), ((), ())), preferred_element_type=F32)
    scores = (scores * dmat_ref[0]).astype(BF16)
    state = state_ref[...]
    o = jnp.dot(scores, v, preferred_element_type=F32)
    o = o + jnp.dot(qdec, state.astype(BF16), preferred_element_type=F32)
    kv = jnp.dot(kdec.T.astype(BF16), v, preferred_element_type=F32)
    state_ref[...] = state * cd_ref[0] + kv

    mu = jnp.mean(o, axis=-1, keepdims=True)
    oc = o - mu
    on = oc * lax.rsqrt(jnp.mean(oc * oc, axis=-1, keepdims=True) + EPS)
    g = g_ref[0].astype(F32)
    o_ref[0] = (g * jax.nn.sigmoid(g) * on).astype(o_ref.dtype)


def _retention_tables(tblk):
    h = jnp.arange(RET_HEADS, dtype=F32)
    log_gamma = jnp.log(1.0 - 2.0 ** (-5.0 - h))
    idx = jnp.arange(tblk, dtype=F32)
    ci = jnp.arange(tblk) // CHUNK
    diff = idx[:, None] - idx[None, :]
    same = ci[:, None] == ci[None, :]
    earlier = ci[None, :] < ci[:, None]
    expo = jnp.where(same, jnp.abs(diff), diff)
    dmat = jnp.where((same | earlier)[None], jnp.exp(log_gamma[:, None, None] * expo[None]), 0.0)
    qd = jnp.exp((idx[None, :] + 1.0) * log_gamma[:, None])
    kd = jnp.exp((tblk - 1.0 - idx[None, :]) * log_gamma[:, None])
    cd = jnp.exp(tblk * log_gamma)
    bc = lambda a: jnp.broadcast_to(a[:, :, None], (RET_HEADS, tblk, HALF_W))
    cdv = jnp.broadcast_to(cd[:, None, None], (RET_HEADS, 1, HEAD_W))
    return dmat, bc(qd), bc(kd), cdv


def retention(proj, cos, sin, tblk=256):
    b, s, _ = proj.shape
    tblk = _pick(s, tblk)
    dmat, qd, kd, cdv = _retention_tables(tblk)
    nh = RET_HEADS

    def col(off):
        return pl.BlockSpec((1, tblk, HEAD_W), lambda bi, h, t: (bi, t, off + h))

    def per_head(shape):
        return pl.BlockSpec((1,) + shape, lambda bi, h, t: (h, 0, 0))

    return pl.pallas_call(
        _retention_kernel,
        grid=(b, nh, s // tblk),
        in_specs=[col(0), col(nh), col(2 * nh), col(3 * nh),
                  pl.BlockSpec((tblk, HALF_W), lambda bi, h, t: (t, 0)),
                  pl.BlockSpec((tblk, HALF_W), lambda bi, h, t: (t, 0)),
                  per_head((tblk, tblk)), per_head((tblk, HALF_W)), per_head((tblk, HALF_W)),
                  per_head((1, HEAD_W))],
        out_specs=pl.BlockSpec((1, tblk, HEAD_W), lambda bi, h, t: (bi, t, h)),
        out_shape=jax.ShapeDtypeStruct((b, s, nh * HEAD_W), BF16),
        scratch_shapes=[pltpu.VMEM((HEAD_W, HEAD_W), F32)],
        compiler_params=_params(("parallel", "parallel", "arbitrary")),
        name="retention",
    )(proj, proj, proj, proj, cos, sin, dmat, qd, kd, cdv)


def _mem_attn_kernel(q_ref, kv_ref, o_ref):
    scale = HEAD_W ** -0.5
    width = MEM_HEADS * HEAD_W
    for h in range(MEM_HEADS):
        q = q_ref[0, :, h * HEAD_W:(h + 1) * HEAD_W]
        mk = kv_ref[0, :, h * HEAD_W:(h + 1) * HEAD_W]
        mv = kv_ref[0, :, width + h * HEAD_W:width + (h + 1) * HEAD_W]
        logits = lax.dot_general(q, mk, (((1,), (1,)), ((), ())), preferred_element_type=F32) * scale
        mx = jnp.max(logits, axis=-1, keepdims=True)
        p = jnp.exp(logits - mx)
        p = p / jnp.sum(p, axis=-1, keepdims=True)
        o = jnp.dot(p.astype(BF16), mv, preferred_element_type=F32)
        o_ref[0, :, h * HEAD_W:(h + 1) * HEAD_W] = o.astype(o_ref.dtype)


def mem_attention(proj, mkv, col_block, tq=512):
    b, s, _ = proj.shape
    mt = mkv.shape[1]
    width = MEM_HEADS * HEAD_W
    tq = _pick(s, tq)
    return pl.pallas_call(
        _mem_attn_kernel,
        grid=(b, s // tq),
        in_specs=[pl.BlockSpec((1, tq, width), lambda bi, t: (bi, t, col_block)),
                  pl.BlockSpec((1, mt, 2 * width), lambda bi, t: (bi, 0, 0))],
        out_specs=pl.BlockSpec((1, tq, width), lambda bi, t: (bi, t, 0)),
        out_shape=jax.ShapeDtypeStruct((b, s, width), BF16),
        compiler_params=_params(("parallel", "parallel")),
        name="mem_attention",
    )(proj, mkv)


DIFF_TILE = 256


def _diff_attn_kernel(q_ref, k_ref, v_ref, bias_ref, lq1_ref, lk1_ref, lq2_ref, lk2_ref, subln_ref,
                      o_ref, acc_ref, *, lam_init):
    t = DIFF_TILE
    qb = pl.program_id(2)
    scale = HALF_W ** -0.5
    q = q_ref[0].astype(F32) * scale
    qs = (q[:, :HALF_W].astype(BF16), q[:, HALF_W:].astype(BF16))

    def logits(kb, half):
        kt = k_ref[0, pl.ds(kb * t, t), half * HALF_W:(half + 1) * HALF_W]
        return lax.dot_general(qs[half], kt, (((1,), (1,)), ((), ())), preferred_element_type=F32)

    def update(kb, s, half, m, l):
        m_new = jnp.maximum(m, jnp.max(s, axis=-1, keepdims=True))
        alpha = jnp.exp(m - m_new)
        p = jnp.exp(s - m_new)
        l_new = alpha * l + jnp.sum(p, axis=-1, keepdims=True)
        v = v_ref[0, pl.ds(kb * t, t), :]
        pv = jnp.dot(p.astype(BF16), v, preferred_element_type=F32)
        acc_ref[half] = alpha * acc_ref[half] + pv
        return m_new, l_new

    acc_ref[...] = jnp.zeros_like(acc_ref)
    neg = jnp.full((t, 1), -jnp.inf, F32)
    zero = jnp.zeros((t, 1), F32)
    carry = (neg, zero, neg, zero)

    def far_body(kb, c):
        m1, l1, m2, l2 = c
        m1, l1 = update(kb, logits(kb, 0), 0, m1, l1)
        m2, l2 = update(kb, logits(kb, 1), 1, m2, l2)
        return m1, l1, m2, l2

    carry = lax.fori_loop(0, jnp.maximum(qb - 1, 0), far_body, carry)

    def near(kb, slot, c, masked):
        m1, l1, m2, l2 = c
        bias = bias_ref[0, slot]
        s1 = logits(kb, 0) + bias
        s2 = logits(kb, 1) + bias
        if masked:
            qc = lax.broadcasted_iota(jnp.int32, (t, t), 0) // CHUNK
            kc = lax.broadcasted_iota(jnp.int32, (t, t), 1) // CHUNK
            ok = kc <= qc
            s1 = jnp.where(ok, s1, -jnp.inf)
            s2 = jnp.where(ok, s2, -jnp.inf)
        m1, l1 = update(kb, s1, 0, m1, l1)
        m2, l2 = update(kb, s2, 1, m2, l2)
        return m1, l1, m2, l2

    carry = lax.cond(qb > 0, lambda c: near(qb - 1, 1, c, False), lambda c: c, carry)
    m1, l1, m2, l2 = near(qb, 0, carry, True)

    lam = (jnp.exp(jnp.sum(lq1_ref[...] * lk1_ref[...])) - jnp.exp(jnp.sum(lq2_ref[...] * lk2_ref[...]))
           + lam_init)
    o = acc_ref[0] / l1 - lam * (acc_ref[1] / l2)
    o = o * lax.rsqrt(jnp.mean(o * o, axis=-1, keepdims=True) + EPS) * subln_ref[...]
    o_ref[0] = (o * (1.0 - lam_init)).astype(o_ref.dtype)


def _t5_bucket(rel):
    half = NUM_BUCKETS // 2
    max_exact = half // 2
    ret = jnp.where(rel > 0, half, 0)
    n = jnp.abs(rel)
    nf = jnp.maximum(n, 1).astype(F32)
    large = max_exact + (jnp.log(nf / max_exact) / math.log(MAX_DISTANCE / max_exact)
                         * (half - max_exact)).astype(jnp.int32)
    large = jnp.minimum(large, half - 1)
    return ret + jnp.where(n < max_exact, n, large)


def _near_bias(rel_bias):
    t = DIFF_TILE
    assert t >= MAX_DISTANCE
    i = jnp.arange(t)
    rel = i[None, :] - i[:, None]
    rel = jnp.stack([rel, rel - t])
    table = rel_bias.astype(F32)
    far = table[NUM_BUCKETS // 2 - 1]
    tiles = table[_t5_bucket(rel)] - far
    return jnp.transpose(tiles, (3, 0, 1, 2))


def diff_attention(qproj, kv, rel_bias, lq1, lk1, lq2, lk2, subln_g, lam_init):
    b, s, _ = qproj.shape
    nh = DIFF_HEADS
    t = DIFF_TILE
    assert s % t == 0
    bias = _near_bias(rel_bias)
    vec = lambda a: a.astype(F32).reshape(1, -1)
    small = lambda n: pl.BlockSpec((1, n), lambda bi, h, qb: (0, 0))
    return pl.pallas_call(
        functools.partial(_diff_attn_kernel, lam_init=lam_init),
        grid=(b, nh, s // t),
        in_specs=[pl.BlockSpec((1, t, HEAD_W), lambda bi, h, qb: (bi, qb, h)),
                  pl.BlockSpec((1, s, HEAD_W), lambda bi, h, qb: (bi, 0, h)),
                  pl.BlockSpec((1, s, HEAD_W), lambda bi, h, qb: (bi, 0, nh + h)),
                  pl.BlockSpec((1, 2, t, t), lambda bi, h, qb: (h, 0, 0, 0)),
                  small(HALF_W), small(HALF_W), small(HALF_W), small(HALF_W), small(HEAD_W)],
        out_specs=pl.BlockSpec((1, t, HEAD_W), lambda bi, h, qb: (bi, qb, h)),
        out_shape=jax.ShapeDtypeStruct((b, s, nh * HEAD_W), BF16),
        scratch_shapes=[pltpu.VMEM((2, t, HEAD_W), F32)],
        compiler_params=_params(("parallel", "parallel", "arbitrary")),
        name="diff_attention",
    )(qproj, kv, kv, bias, vec(lq1), vec(lk1), vec(lq2), vec(lk2), vec(subln_g))


def _up_conv_gate_kernel(h_ref, wa_ref, wg_ref, cwa_ref, cwg_ref, cba_ref, cbg_ref, o_ref,
                         halo_a, halo_g, *, tiles_per_seq):
    first = (pl.program_id(1) % tiles_per_seq) == 0
    h = h_ref[...]
    bm = h.shape[0]

    def conv(w_ref, cw_ref, cb_ref, halo_ref):
        u = jnp.dot(h, w_ref[...], preferred_element_type=F32)
        prev = jnp.where(first, 0.0, halo_ref[...])
        row = lax.broadcasted_iota(jnp.int32, u.shape, 0)
        u1 = jnp.where(row == 0, prev[7:8], pltpu.roll(u, 1, 0))
        u2 = jnp.where(row == 0, prev[6:7], jnp.where(row == 1, prev[7:8], pltpu.roll(u, 2, 0)))
        halo_ref[...] = u[bm - 8:]
        cw = cw_ref[...]
        return cb_ref[...] + cw[0:1] * u2 + cw[1:2] * u1 + cw[2:3] * u

    a = conv(wa_ref, cwa_ref, cba_ref, halo_a)
    g = conv(wg_ref, cwg_ref, cbg_ref, halo_g)
    o_ref[...] = (a * jax.nn.sigmoid(a) * g).astype(o_ref.dtype)


def up_conv_gate(h, w_up, conv_w, conv_b, seq, bm=512, bn=512):
    m, d = h.shape
    f = w_up.shape[1] // 2
    bm, bn = _pick(seq, bm), _pick(f, bn)
    nj = f // bn
    return pl.pallas_call(
        functools.partial(_up_conv_gate_kernel, tiles_per_seq=seq // bm),
        grid=(nj, m // bm),
        in_specs=[pl.BlockSpec((bm, d), lambda j, i: (i, 0)),
                  pl.BlockSpec((d, bn), lambda j, i: (0, j)),
                  pl.BlockSpec((d, bn), lambda j, i: (0, nj + j)),
                  pl.BlockSpec((3, bn), lambda j, i: (0, j)),
                  pl.BlockSpec((3, bn), lambda j, i: (0, nj + j)),
                  pl.BlockSpec((1, bn), lambda j, i: (0, j)),
                  pl.BlockSpec((1, bn), lambda j, i: (0, nj + j))],
        out_specs=pl.BlockSpec((bm, bn), lambda j, i: (i, j)),
        out_shape=jax.ShapeDtypeStruct((m, f), BF16),
        scratch_shapes=[pltpu.VMEM((8, bn), F32), pltpu.VMEM((8, bn), F32)],
        compiler_params=_params(("parallel", "arbitrary")),
        name="up_conv_gate",
    )(h, w_up, w_up, conv_w, conv_w, conv_b, conv_b)


def kernel(x, mem, a_w_in, b_w_q, w_kv, w_out, mem_w_kv, ffn_w_up, ffn_conv_w, ffn_conv_b, ffn_w_down,
           attn_norm, ffn_norm, kv_norm, mem_norm, final_norm, rel_bias,
           lambda_q1, lambda_k1, lambda_q2, lambda_k2, diff_subln):
    b, s, d = x.shape
    depth = w_out.shape[0]
    n_a = a_w_in.shape[0]
    m = b * s
    mix_w = RET_HEADS * HEAD_W
    mem_w = MEM_HEADS * HEAD_W
    assert d == mix_w + mem_w and a_w_in.shape[2] == 4 * mix_w + mem_w and b_w_q.shape[2] == mix_w + mem_w

    pos = jnp.arange(s, dtype=F32)
    inv_freq = ROPE_BASE ** (-jnp.arange(HALF_W, dtype=F32) / HALF_W)
    ang = pos[:, None] * inv_freq[None, :]
    cos, sin = jnp.cos(ang), jnp.sin(ang)

    xf = x.reshape(m, d)
    memn = rmsnorm(mem.reshape(-1, d), mem_norm, BF16)
    kv = None
    for layer in range(depth):
        if layer == n_a:
            kv = matmul(rmsnorm(xf, kv_norm, BF16), w_kv.astype(BF16), BF16).reshape(b, s, -1)
        h = rmsnorm(xf, attn_norm[layer], BF16)
        mkv = matmul(memn, mem_w_kv[layer].astype(BF16), BF16).reshape(b, -1, 2 * mem_w)
        if layer < n_a:
            proj = matmul(h, a_w_in[layer].astype(BF16), BF16).reshape(b, s, -1)
            mix = retention(proj, cos, sin)
            mo = mem_attention(proj, mkv, (4 * mix_w) // mem_w)
        else:
            bl = layer - n_a
            lam_init = 0.8 - 0.6 * math.exp(-0.3 * layer)
            proj = matmul(h, b_w_q[bl].astype(BF16), BF16).reshape(b, s, -1)
            mix = diff_attention(proj, kv, rel_bias, lambda_q1[bl], lambda_k1[bl], lambda_q2[bl],
                                 lambda_k2[bl], diff_subln[bl], lam_init)
            mo = mem_attention(proj, mkv, mix_w // mem_w)
        xf = out_proj(mix.reshape(m, mix_w), mo.reshape(m, mem_w), w_out[layer].astype(BF16), xf)
        h = rmsnorm(xf, ffn_norm[layer], BF16)
        act = up_conv_gate(h, ffn_w_up[layer].astype(BF16), ffn_conv_w[layer],
                           ffn_conv_b[layer].reshape(1, -1), s)
        xf = matmul(act, ffn_w_down[layer].astype(BF16), F32, resid=xf, bm=512, bn=512)
    return rmsnorm(xf, final_norm, F32).reshape(b, s, d)
```

```python
import functools
import math

import jax
import jax.numpy as jnp
from jax import lax
from jax.experimental import pallas as pl
from jax.experimental.pallas import tpu as pltpu

F32 = jnp.float32
BF16 = jnp.bfloat16

EPS = 1e-6
CHUNK = 64
ROPE_BASE = 10000.0
NUM_BUCKETS = 32
MAX_DISTANCE = 128
MEM_HEADS = 4
RET_HEADS = 12
DIFF_HEADS = 12
HEAD_W = 256
HALF_W = HEAD_W // 2

VMEM_LIMIT = 56 * 1024 * 1024


def _params(sem):
    return pltpu.CompilerParams(dimension_semantics=sem, vmem_limit_bytes=VMEM_LIMIT)


def _pick(n, pref):
    b = min(pref, n)
    while n % b:
        b //= 2
    return b


def _rmsnorm_kernel(x_ref, g_ref, o_ref):
    x = x_ref[...]
    ms = jnp.mean(x * x, axis=-1, keepdims=True)
    o_ref[...] = (x * lax.rsqrt(ms + EPS) * g_ref[...]).astype(o_ref.dtype)


def rmsnorm(x, g, out_dtype, bm=256):
    m, d = x.shape
    bm = _pick(m, bm)
    return pl.pallas_call(
        _rmsnorm_kernel,
        grid=(m // bm,),
        in_specs=[pl.BlockSpec((bm, d), lambda i: (i, 0)),
                  pl.BlockSpec((1, d), lambda i: (0, 0))],
        out_specs=pl.BlockSpec((bm, d), lambda i: (i, 0)),
        out_shape=jax.ShapeDtypeStruct((m, d), out_dtype),
        compiler_params=_params(("parallel",)),
        name="rmsnorm",
    )(x, g.reshape(1, d))


def _mm_kernel(*refs, n_act, has_resid):
    a_refs, w_refs = refs[:n_act], refs[n_act:2 * n_act]
    o_ref = refs[-1]
    acc = None
    for a_ref, w_ref in zip(a_refs, w_refs):
        part = jnp.dot(a_ref[...], w_ref[...].astype(BF16), preferred_element_type=F32)
        acc = part if acc is None else acc + part
    if has_resid:
        acc = refs[2 * n_act][...] + acc
    o_ref[...] = acc.astype(o_ref.dtype)


def matmul(acts, w, layer, out_dtype, resid=None, bm=2048, bn=512, name="matmul"):
    m = acts[0].shape[0]
    n = w.shape[2]
    bm, bn = _pick(m, bm), _pick(n, bn)
    act_specs, w_specs, off = [], [], 0
    for a in acts:
        ki = a.shape[1]
        assert off % ki == 0
        act_specs.append(pl.BlockSpec((bm, ki), lambda i, j: (i, 0), pipeline_mode=pl.Buffered(1)))
        w_specs.append(pl.BlockSpec((None, ki, bn), lambda i, j, kb=off // ki: (layer, kb, j)))
        off += ki
    assert off == w.shape[1]
    in_specs = act_specs + w_specs
    args = list(acts) + [w] * len(acts)
    if resid is not None:
        in_specs.append(pl.BlockSpec((bm, bn), lambda i, j: (i, j)))
        args.append(resid)
    return pl.pallas_call(
        functools.partial(_mm_kernel, n_act=len(acts), has_resid=resid is not None),
        grid=(m // bm, n // bn),
        in_specs=in_specs,
        out_specs=pl.BlockSpec((bm, bn), lambda i, j: (i, j)),
        out_shape=jax.ShapeDtypeStruct((m, n), out_dtype),
        compiler_params=_params(("parallel", "arbitrary")),
        name=name,
    )(*args)


def _retention_kernel(q_ref, k_ref, v_ref, g_ref, cos_ref, sin_ref, dmat_ref, qd_ref, kd_ref, cd_ref,
                      o_ref, state_ref):
    @pl.when(pl.program_id(2) == 0)
    def _():
        state_ref[...] = jnp.zeros_like(state_ref)

    cos = cos_ref[...]
    sin = sin_ref[...]

    def rot(x):
        x1, x2 = x[:, :HALF_W], x[:, HALF_W:]
        return x1 * cos - x2 * sin, x2 * cos + x1 * sin

    q1, q2 = rot(q_ref[0].astype(F32))
    k1, k2 = rot(k_ref[0].astype(F32))
    kscale = HEAD_W ** -0.5
    k1, k2 = k1 * kscale, k2 * kscale
    v = v_ref[0]

    qd = qd_ref[0]
    kd = kd_ref[0]
    qr = jnp.concatenate([q1, q2], axis=-1).astype(BF16)
    kr = jnp.concatenate([k1, k2], axis=-1).astype(BF16)
    qdec = jnp.concatenate([q1 * qd, q2 * qd], axis=-1).astype(BF16)
    kdec = jnp.concatenate([k1 * kd, k2 * kd], axis=-1)

    scores = lax.dot_general(qr, kr, (((1,), (1,)), ((), ())), preferred_element_type=F32)
    scores = (scores * dmat_ref[0]).astype(BF16)
    state = state_ref[...]
    o = jnp.dot(scores, v, preferred_element_type=F32)
    o = o + jnp.dot(qdec, state.astype(BF16), preferred_element_type=F32)
    kv = jnp.dot(kdec.T.astype(BF16), v, preferred_element_type=F32)
    state_ref[...] = state * cd_ref[0] + kv

    mu = jnp.mean(o, axis=-1, keepdims=True)
    oc = o - mu
    on = oc * lax.rsqrt(jnp.mean(oc * oc, axis=-1, keepdims=True) + EPS)
    g = g_ref[0].astype(F32)
    o_ref[0] = (g * jax.nn.sigmoid(g) * on).astype(o_ref.dtype)


def _retention_tables(tblk):
    h = jnp.arange(RET_HEADS, dtype=F32)
    log_gamma = jnp.log(1.0 - 2.0 ** (-5.0 - h))
    idx = jnp.arange(tblk, dtype=F32)
    ci = jnp.arange(tblk) // CHUNK
    diff = idx[:, None] - idx[None, :]
    same = ci[:, None] == ci[None, :]
    earlier = ci[None, :] < ci[:, None]
    expo = jnp.where(same, jnp.abs(diff), diff)
    dmat = jnp.where((same | earlier)[None], jnp.exp(log_gamma[:, None, None] * expo[None]), 0.0)
    qd = jnp.exp((idx[None, :] + 1.0) * log_gamma[:, None])
    kd = jnp.exp((tblk - 1.0 - idx[None, :]) * log_gamma[:, None])
    cd = jnp.exp(tblk * log_gamma)
    bc = lambda a: jnp.broadcast_to(a[:, :, None], (RET_HEADS, tblk, HALF_W))
    cdv = jnp.broadcast_to(cd[:, None, None], (RET_HEADS, 1, HEAD_W))
    return dmat, bc(qd), bc(kd), cdv


def retention(proj, cos, sin, tblk=256):
    b, s, _ = proj.shape
    tblk = _pick(s, tblk)
    dmat, qd, kd, cdv = _retention_tables(tblk)
    nh = RET_HEADS

    def col(off):
        return pl.BlockSpec((1, tblk, HEAD_W), lambda bi, h, t: (bi, t, off + h))

    def per_head(shape):
        return pl.BlockSpec((1,) + shape, lambda bi, h, t: (h, 0, 0))

    return pl.pallas_call(
        _retention_kernel,
        grid=(b, nh, s // tblk),
        in_specs=[col(0), col(nh), col(2 * nh), col(3 * nh),
                  pl.BlockSpec((tblk, HALF_W), lambda bi, h, t: (t, 0)),
                  pl.BlockSpec((tblk, HALF_W), lambda bi, h, t: (t, 0)),
                  per_head((tblk, tblk)), per_head((tblk, HALF_W)), per_head((tblk, HALF_W)),
                  per_head((1, HEAD_W))],
        out_specs=pl.BlockSpec((1, tblk, HEAD_W), lambda bi, h, t: (bi, t, h)),
        out_shape=jax.ShapeDtypeStruct((b, s, nh * HEAD_W), BF16),
        scratch_shapes=[pltpu.VMEM((HEAD_W, HEAD_W), F32)],
        compiler_params=_params(("parallel", "parallel", "arbitrary")),
        name="retention",
    )(proj, proj, proj, proj, cos, sin, dmat, qd, kd, cdv)


def _mem_attn_kernel(q_ref, kv_ref, o_ref):
    scale = HEAD_W ** -0.5
    width = MEM_HEADS * HEAD_W
    for h in range(MEM_HEADS):
        q = q_ref[0, :, h * HEAD_W:(h + 1) * HEAD_W]
        mk = kv_ref[0, :, h * HEAD_W:(h + 1) * HEAD_W]
        mv = kv_ref[0, :, width + h * HEAD_W:width + (h + 1) * HEAD_W]
        logits = lax.dot_general(q, mk, (((1,), (1,)), ((), ())), preferred_element_type=F32) * scale
        mx = jnp.max(logits, axis=-1, keepdims=True)
        p = jnp.exp(logits - mx)
        p = p / jnp.sum(p, axis=-1, keepdims=True)
        o = jnp.dot(p.astype(BF16), mv, preferred_element_type=F32)
        o_ref[0, :, h * HEAD_W:(h + 1) * HEAD_W] = o.astype(o_ref.dtype)


def mem_attention(proj, mkv, col_block, tq=512):
    b, s, _ = proj.shape
    mt = mkv.shape[1]
    width = MEM_HEADS * HEAD_W
    tq = _pick(s, tq)
    return pl.pallas_call(
        _mem_attn_kernel,
        grid=(b, s // tq),
        in_specs=[pl.BlockSpec((1, tq, width), lambda bi, t: (bi, t, col_block)),
                  pl.BlockSpec((1, mt, 2 * width), lambda bi, t: (bi, 0, 0))],
        out_specs=pl.BlockSpec((1, tq, width), lambda bi, t: (bi, t, 0)),
        out_shape=jax.ShapeDtypeStruct((b, s, width), BF16),
        compiler_params=_params(("parallel", "parallel")),
        name="mem_attention",
    )(proj, mkv)


DIFF_TILE = 512
LOG2E = math.log2(math.e)


def _t5_bucket(rel):
    half = NUM_BUCKETS // 2
    max_exact = half // 2
    ret = jnp.where(rel > 0, half, 0)
    n = jnp.abs(rel)
    nf = jnp.maximum(n, 1).astype(F32)
    large = max_exact + (jnp.log(nf / max_exact) / math.log(MAX_DISTANCE / max_exact)
                         * (half - max_exact)).astype(jnp.int32)
    large = jnp.minimum(large, half - 1)
    return ret + jnp.where(n < max_exact, n, large)


def _bias_buckets(t):
    i = jnp.arange(t)
    rel = i[:, None] - i[None, :]
    bucket = _t5_bucket(jnp.stack([rel, rel - t]))
    masked = (i[:, None] // CHUNK) > (i[None, :] // CHUNK)
    return bucket.at[0].set(jnp.where(masked, NUM_BUCKETS, bucket[0]))


def _bias_kernel(table_ref, bucket_ref, o_ref):
    h = pl.program_id(0)
    far = table_ref[NUM_BUCKETS // 2 - 1, h]
    rows = 64

    def body(c, carry):
        for slot in range(2):
            b = bucket_ref[slot, pl.ds(c * rows, rows), :]
            acc = jnp.full(b.shape, -jnp.inf, F32)
            for i in range(NUM_BUCKETS):
                acc = jnp.where(b == i, (table_ref[i, h] - far) * LOG2E, acc)
            o_ref[0, slot, pl.ds(c * rows, rows), :] = acc
        return carry

    lax.fori_loop(0, bucket_ref.shape[1] // rows, body, 0)


def bias_tiles(rel_bias):
    t = DIFF_TILE
    assert t >= MAX_DISTANCE and t % CHUNK == 0
    nh = rel_bias.shape[1]
    return pl.pallas_call(
        _bias_kernel,
        grid=(nh,),
        in_specs=[pl.BlockSpec(memory_space=pltpu.SMEM),
                  pl.BlockSpec((2, t, t), lambda h: (0, 0, 0))],
        out_specs=pl.BlockSpec((1, 2, t, t), lambda h: (h, 0, 0, 0)),
        out_shape=jax.ShapeDtypeStruct((nh, 2, t, t), F32),
        compiler_params=_params(("parallel",)),
        name="bias_tiles",
    )(rel_bias.astype(F32), _bias_buckets(t))


def _diff_attn_kernel(q_ref, k_ref, vt_ref, bias_ref, lq1_ref, lk1_ref, lq2_ref, lk2_ref, subln_ref,
                      o_ref, acc_ref, *, lam_init):
    t = DIFF_TILE
    i = pl.program_id(2)
    qt = (q_ref[0].astype(F32) * (HALF_W ** -0.5 * LOG2E)).T
    qts = (qt[:HALF_W].astype(BF16), qt[HALF_W:].astype(BF16))
    acc_ref[...] = jnp.zeros_like(acc_ref)

    def process(items, carry):
        scores = []
        for kb, _ in items:
            kc = k_ref[0, pl.ds(kb * t, t), :]
            scores.append([jnp.dot(kc[:, half * HALF_W:(half + 1) * HALF_W], qts[half],
                                   preferred_element_type=F32) for half in range(2)])
        m, l = [carry[0], carry[2]], [carry[1], carry[3]]
        for (kb, slot), s_pair in zip(items, scores):
            vt = vt_ref[0, 0, kb]
            for half in range(2):
                s = s_pair[half]
                if slot is not None:
                    s = s + bias_ref[0, slot]
                m_new = jnp.maximum(m[half], jnp.max(s, axis=0, keepdims=True))
                alpha = jnp.exp2(m[half] - m_new)
                p = jnp.exp2(s - m_new)
                l[half] = alpha * l[half] + jnp.sum(p, axis=0, keepdims=True)
                pv = jnp.dot(vt, p.astype(BF16), preferred_element_type=F32)
                acc_ref[half] = alpha * acc_ref[half] + pv
                m[half] = m_new
        return m[0], l[0], m[1], l[1]

    neg = jnp.full((1, t), -jnp.inf, F32)
    zero = jnp.zeros((1, t), F32)
    n_far = jnp.maximum(i - 1, 0)
    odd = n_far % 2
    carry = lax.cond(odd == 1, lambda c: process([(0, None)], c), lambda c: c, (neg, zero, neg, zero))
    carry = lax.fori_loop(
        0, n_far // 2, lambda j, c: process([(odd + 2 * j, None), (odd + 2 * j + 1, None)], c), carry)
    m1, l1, m2, l2 = lax.cond(i >= 1, lambda c: process([(i - 1, 1), (i, 0)], c),
                              lambda c: process([(0, 0)], c), carry)

    lam = (jnp.exp(jnp.sum(lq1_ref[...] * lk1_ref[...])) - jnp.exp(jnp.sum(lq2_ref[...] * lk2_ref[...]))
           + lam_init)
    o = acc_ref[0] * (1.0 / l1) - acc_ref[1] * (lam / l2)
    o = o * lax.rsqrt(jnp.mean(o * o, axis=0, keepdims=True) + EPS) * subln_ref[...]
    o_ref[0] = (o * (1.0 - lam_init)).T.astype(o_ref.dtype)


def diff_attention(qproj, kv, bias, lq1, lk1, lq2, lk2, subln_g, lam_init):
    b, s, _ = qproj.shape
    nh = DIFF_HEADS
    t = DIFF_TILE
    assert s % t == 0
    nc = s // t
    vt = kv[:, :, nh * HEAD_W:].reshape(b, nc, t, nh, HEAD_W).transpose(0, 3, 1, 4, 2)
    vec = lambda a: a.astype(F32).reshape(1, -1)
    small = lambda n: pl.BlockSpec((1, n), lambda bi, h, qb: (0, 0))
    return pl.pallas_call(
        functools.partial(_diff_attn_kernel, lam_init=lam_init),
        grid=(b, nh, nc),
        in_specs=[pl.BlockSpec((1, t, HEAD_W), lambda bi, h, qb: (bi, qb, h)),
                  pl.BlockSpec((1, s, HEAD_W), lambda bi, h, qb: (bi, 0, h)),
                  pl.BlockSpec((1, 1, nc, HEAD_W, t), lambda bi, h, qb: (bi, h, 0, 0, 0)),
                  pl.BlockSpec((1, 2, t, t), lambda bi, h, qb: (h, 0, 0, 0)),
                  small(HALF_W), small(HALF_W), small(HALF_W), small(HALF_W),
                  pl.BlockSpec((HEAD_W, 1), lambda bi, h, qb: (0, 0))],
        out_specs=pl.BlockSpec((1, t, HEAD_W), lambda bi, h, qb: (bi, qb, h)),
        out_shape=jax.ShapeDtypeStruct((b, s, nh * HEAD_W), BF16),
        scratch_shapes=[pltpu.VMEM((2, HEAD_W, t), F32)],
        compiler_params=_params(("parallel", "parallel", "arbitrary")),
        name="diff_attention",
    )(qproj, kv, vt, bias, vec(lq1), vec(lk1), vec(lq2), vec(lk2), subln_g.astype(F32).reshape(-1, 1))


def _up_conv_gate_kernel(h_ref, wa_ref, wg_ref, cwa_ref, cwg_ref, cba_ref, cbg_ref, o_ref,
                         halo_a, halo_g, *, tiles_per_seq, rc):
    first = (pl.program_id(1) % tiles_per_seq) == 0
    bm = h_ref.shape[0]
    wa, wg = wa_ref[...], wg_ref[...]

    us = []
    for c in range(bm // rc):
        hc = h_ref[c * rc:(c + 1) * rc, :]
        us.append((jnp.dot(hc, wa, preferred_element_type=F32), jnp.dot(hc, wg, preferred_element_type=F32)))

    row8 = lax.broadcasted_iota(jnp.int32, (8, wa.shape[1]), 0)

    def conv(u, prev, cw_ref, cb_ref):
        top = u[:8]
        t1 = jnp.where(row8 == 0, prev[7:8], pltpu.roll(top, 1, 0))
        t2 = jnp.where(row8 == 0, prev[6:7], jnp.where(row8 == 1, prev[7:8], pltpu.roll(top, 2, 0)))
        u1 = jnp.concatenate([t1, pltpu.roll(u, 1, 0)[8:]], axis=0)
        u2 = jnp.concatenate([t2, pltpu.roll(u, 2, 0)[8:]], axis=0)
        cw = cw_ref[...]
        return cb_ref[...] + cw[0:1] * u2 + cw[1:2] * u1 + cw[2:3] * u

    prev_a = jnp.where(first, 0.0, halo_a[...])
    prev_g = jnp.where(first, 0.0, halo_g[...])
    for c, (ua, ug) in enumerate(us):
        a = conv(ua, prev_a, cwa_ref, cba_ref)
        g = conv(ug, prev_g, cwg_ref, cbg_ref)
        o_ref[c * rc:(c + 1) * rc, :] = (a * jax.nn.sigmoid(a) * g).astype(o_ref.dtype)
        prev_a, prev_g = ua[rc - 8:], ug[rc - 8:]
    halo_a[...] = prev_a
    halo_g[...] = prev_g


def up_conv_gate(h, w_up, conv_w, conv_b, seq, bm=1024, bn=512, rc=256):
    m, d = h.shape
    f = w_up.shape[1] // 2
    bm, bn = _pick(seq, bm), _pick(f, bn)
    nj = f // bn
    return pl.pallas_call(
        functools.partial(_up_conv_gate_kernel, tiles_per_seq=seq // bm, rc=_pick(bm, rc)),
        grid=(nj, m // bm),
        in_specs=[pl.BlockSpec((bm, d), lambda j, i: (i, 0)),
                  pl.BlockSpec((d, bn), lambda j, i: (0, j)),
                  pl.BlockSpec((d, bn), lambda j, i: (0, nj + j)),
                  pl.BlockSpec((3, bn), lambda j, i: (0, j)),
                  pl.BlockSpec((3, bn), lambda j, i: (0, nj + j)),
                  pl.BlockSpec((1, bn), lambda j, i: (0, j)),
                  pl.BlockSpec((1, bn), lambda j, i: (0, nj + j))],
        out_specs=pl.BlockSpec((bm, bn), lambda j, i: (i, j)),
        out_shape=jax.ShapeDtypeStruct((m, f), BF16),
        scratch_shapes=[pltpu.VMEM((8, bn), F32), pltpu.VMEM((8, bn), F32)],
        compiler_params=_params(("parallel", "arbitrary")),
        name="up_conv_gate",
    )(h, w_up, w_up, conv_w, conv_w, conv_b, conv_b)


def kernel(x, mem, a_w_in, b_w_q, w_kv, w_out, mem_w_kv, ffn_w_up, ffn_conv_w, ffn_conv_b, ffn_w_down,
           attn_norm, ffn_norm, kv_norm, mem_norm, final_norm, rel_bias,
           lambda_q1, lambda_k1, lambda_q2, lambda_k2, diff_subln):
    b, s, d = x.shape
    depth = w_out.shape[0]
    n_a = a_w_in.shape[0]
    m = b * s
    mix_w = RET_HEADS * HEAD_W
    mem_w = MEM_HEADS * HEAD_W
    assert d == mix_w + mem_w and a_w_in.shape[2] == 4 * mix_w + mem_w and b_w_q.shape[2] == mix_w + mem_w

    pos = jnp.arange(s, dtype=F32)
    inv_freq = ROPE_BASE ** (-jnp.arange(HALF_W, dtype=F32) / HALF_W)
    ang = pos[:, None] * inv_freq[None, :]
    cos, sin = jnp.cos(ang), jnp.sin(ang)

    xf = x.reshape(m, d)
    memn = rmsnorm(mem.reshape(-1, d), mem_norm, BF16)
    kv = bias = None
    for layer in range(depth):
        if layer == n_a:
            kv = matmul([rmsnorm(xf, kv_norm, BF16)], w_kv[None], 0, BF16).reshape(b, s, -1)
            bias = bias_tiles(rel_bias)
        h = rmsnorm(xf, attn_norm[layer], BF16)
        mkv = matmul([memn], mem_w_kv, layer, BF16).reshape(b, -1, 2 * mem_w)
        if layer < n_a:
            proj = matmul([h], a_w_in, layer, BF16).reshape(b, s, -1)
            mix = retention(proj, cos, sin)
            mo = mem_attention(proj, mkv, (4 * mix_w) // mem_w)
        else:
            bl = layer - n_a
            lam_init = 0.8 - 0.6 * math.exp(-0.3 * layer)
            proj = matmul([h], b_w_q, bl, BF16).reshape(b, s, -1)
            mix = diff_attention(proj, kv, bias, lambda_q1[bl], lambda_k1[bl], lambda_q2[bl],
                                 lambda_k2[bl], diff_subln[bl], lam_init)
            mo = mem_attention(proj, mkv, mix_w // mem_w)
        xf = matmul([mix.reshape(m, mix_w), mo.reshape(m, mem_w)], w_out, layer, F32, resid=xf,
                    bm=1024, bn=512, name="out_proj")
        h = rmsnorm(xf, ffn_norm[layer], BF16)
        act = up_conv_gate(h, ffn_w_up[layer].astype(BF16), ffn_conv_w[layer],
                           ffn_conv_b[layer].reshape(1, -1), s)
        xf = matmul([act], ffn_w_down, layer, F32, resid=xf, bm=1024, bn=256, name="down_proj")
    return rmsnorm(xf, final_norm, F32).reshape(b, s, d)
```

```python
import functools
import math

import jax
import jax.numpy as jnp
from jax import lax
from jax.experimental import pallas as pl
from jax.experimental.pallas import tpu as pltpu

F32 = jnp.float32
BF16 = jnp.bfloat16

EPS = 1e-6
CHUNK = 64
ROPE_BASE = 10000.0
NUM_BUCKETS = 32
MAX_DISTANCE = 128
MEM_HEADS = 4
RET_HEADS = 12
DIFF_HEADS = 12
HEAD_W = 256
HALF_W = HEAD_W // 2

VMEM_LIMIT = 56 * 1024 * 1024


def _params(sem):
    return pltpu.CompilerParams(dimension_semantics=sem, vmem_limit_bytes=VMEM_LIMIT)


def _pick(n, pref):
    b = min(pref, n)
    while n % b:
        b //= 2
    return b


def _rmsnorm_kernel(x_ref, g_ref, o_ref):
    x = x_ref[...]
    ms = jnp.mean(x * x, axis=-1, keepdims=True)
    o_ref[...] = (x * lax.rsqrt(ms + EPS) * g_ref[...]).astype(o_ref.dtype)


def rmsnorm(x, g, out_dtype, bm=256):
    m, d = x.shape
    bm = _pick(m, bm)
    return pl.pallas_call(
        _rmsnorm_kernel,
        grid=(m // bm,),
        in_specs=[pl.BlockSpec((bm, d), lambda i: (i, 0)),
                  pl.BlockSpec((1, d), lambda i: (0, 0))],
        out_specs=pl.BlockSpec((bm, d), lambda i: (i, 0)),
        out_shape=jax.ShapeDtypeStruct((m, d), out_dtype),
        compiler_params=_params(("parallel",)),
        name="rmsnorm",
    )(x, g.reshape(1, d))


LANES = 128


def _row_rstd(ssq, norm_dim):
    return lax.rsqrt(jnp.sum(ssq, axis=1, keepdims=True) * (1.0 / norm_dim) + EPS)


def _lane_partial_ssq(x):
    sq = x * x
    part = sq[:, :LANES]
    for c in range(1, x.shape[1] // LANES):
        part = part + sq[:, c * LANES:(c + 1) * LANES]
    return part


def _mm_kernel(*refs, n_act, has_resid, has_rowscale, n_gain, norm_dim, rc):
    it = iter(refs)
    a_refs = [next(it) for _ in range(n_act)]
    w_refs = [next(it) for _ in range(n_act)]
    r_ref = next(it) if has_resid else None
    ssq_ref = next(it) if has_rowscale else None
    g_refs = [next(it) for _ in range(n_gain)]
    o_ref = next(it)
    xg_refs = [next(it) for _ in range(n_gain)]
    ssq_out = next(it) if n_gain else None

    bm, bn = o_ref.shape
    ws = [w_ref[...].astype(BF16) for w_ref in w_refs]
    accs = []
    for c in range(bm // rc):
        rows = slice(c * rc, (c + 1) * rc)
        acc = None
        for a_ref, w in zip(a_refs, ws):
            part = jnp.dot(a_ref[rows, :], w, preferred_element_type=F32)
            acc = part if acc is None else acc + part
        accs.append(acc)
    if n_gain:
        @pl.when(pl.program_id(1) == 0)
        def _():
            ssq_out[...] = jnp.zeros_like(ssq_out)

    for c, acc in enumerate(accs):
        rows = slice(c * rc, (c + 1) * rc)
        if has_rowscale:
            acc = acc * _row_rstd(ssq_ref[rows, :], norm_dim)
        if has_resid:
            acc = r_ref[rows, :] + acc
        o_ref[rows, :] = acc.astype(o_ref.dtype)
        for g_ref, xg_ref in zip(g_refs, xg_refs):
            xg_ref[rows, :] = (acc * g_ref[...]).astype(xg_ref.dtype)
        if n_gain:
            ssq_out[rows, :] += _lane_partial_ssq(acc)


def matmul(acts, w, layer, out_dtype, resid=None, ssq=None, gains=(), bm=2048, bn=512, name="matmul"):
    m = acts[0].shape[0]
    k, n = w.shape[1], w.shape[2]
    bm, bn = _pick(m, bm), _pick(n, bn)
    tile = pl.BlockSpec((bm, bn), lambda i, j: (i, j))
    rows = pl.BlockSpec((bm, LANES), lambda i, j: (i, 0))
    act_specs, w_specs, off = [], [], 0
    for a in acts:
        ki = a.shape[1]
        assert off % ki == 0
        act_specs.append(pl.BlockSpec((bm, ki), lambda i, j: (i, 0), pipeline_mode=pl.Buffered(1)))
        w_specs.append(pl.BlockSpec((None, ki, bn), lambda i, j, kb=off // ki: (layer, kb, j)))
        off += ki
    assert off == k
    in_specs = act_specs + w_specs
    args = list(acts) + [w] * len(acts)
    if resid is not None:
        in_specs.append(tile)
        args.append(resid)
    if ssq is not None:
        in_specs.append(rows)
        args.append(ssq)
    for g in gains:
        in_specs.append(pl.BlockSpec((1, bn), lambda i, j: (0, j)))
        args.append(g.reshape(1, n))
    out_specs = [tile] + [tile] * len(gains)
    out_shape = [jax.ShapeDtypeStruct((m, n), out_dtype)] + [jax.ShapeDtypeStruct((m, n), BF16)] * len(gains)
    if gains:
        out_specs.append(rows)
        out_shape.append(jax.ShapeDtypeStruct((m, LANES), F32))
    outs = pl.pallas_call(
        functools.partial(_mm_kernel, n_act=len(acts), has_resid=resid is not None,
                          has_rowscale=ssq is not None, n_gain=len(gains), norm_dim=k,
                          rc=bm),
        grid=(m // bm, n // bn),
        in_specs=in_specs,
        out_specs=out_specs,
        out_shape=out_shape,
        compiler_params=_params(("parallel", "arbitrary")),
        name=name,
    )(*args)
    if not gains:
        return outs[0]
    return outs[0], outs[1:-1], outs[-1]


def _retention_kernel(q_ref, k_ref, v_ref, g_ref, cos_ref, sin_ref, dmat_ref, qd_ref, kd_ref, cd_ref,
                      o_ref, state_ref):
    @pl.when(pl.program_id(2) == 0)
    def _():
        state_ref[...] = jnp.zeros_like(state_ref)

    cos = cos_ref[...]
    sin = sin_ref[...]

    def rot(x):
        x1, x2 = x[:, :HALF_W], x[:, HALF_W:]
        return x1 * cos - x2 * sin, x2 * cos + x1 * sin

    q1, q2 = rot(q_ref[0].astype(F32))
    k1, k2 = rot(k_ref[0].astype(F32))
    kscale = HEAD_W ** -0.5
    k1, k2 = k1 * kscale, k2 * kscale
    v = v_ref[0]

    qd = qd_ref[0]
    kd = kd_ref[0]
    qr = jnp.concatenate([q1, q2], axis=-1).astype(BF16)
    kr = jnp.concatenate([k1, k2], axis=-1).astype(BF16)
    qdec = jnp.concatenate([q1 * qd, q2 * qd], axis=-1).astype(BF16)
    kdec = jnp.concatenate([k1 * kd, k2 * kd], axis=-1)

    scores = lax.dot_general(qr, kr, (((1,), (1,)), ((), ())), preferred_element_type=F32)
    scores = (scores * dmat_ref[0]).astype(BF16)
    state = state_ref[...]
    o = jnp.dot(scores, v, preferred_element_type=F32)
    o = o + jnp.dot(qdec, state.astype(BF16), preferred_element_type=F32)
    kv = jnp.dot(kdec.T.astype(BF16), v, preferred_element_type=F32)
    state_ref[...] = state * cd_ref[0] + kv

    mu = jnp.mean(o, axis=-1, keepdims=True)
    oc = o - mu
    on = oc * lax.rsqrt(jnp.mean(oc * oc, axis=-1, keepdims=True) + EPS)
    g = g_ref[0].astype(F32)
    o_ref[0] = (g * jax.nn.sigmoid(g) * on).astype(o_ref.dtype)


def _retention_tables(tblk):
    h = jnp.arange(RET_HEADS, dtype=F32)
    log_gamma = jnp.log(1.0 - 2.0 ** (-5.0 - h))
    idx = jnp.arange(tblk, dtype=F32)
    ci = jnp.arange(tblk) // CHUNK
    diff = idx[:, None] - idx[None, :]
    same = ci[:, None] == ci[None, :]
    earlier = ci[None, :] < ci[:, None]
    expo = jnp.where(same, jnp.abs(diff), diff)
    dmat = jnp.where((same | earlier)[None], jnp.exp(log_gamma[:, None, None] * expo[None]), 0.0)
    qd = jnp.exp((idx[None, :] + 1.0) * log_gamma[:, None])
    kd = jnp.exp((tblk - 1.0 - idx[None, :]) * log_gamma[:, None])
    cd = jnp.exp(tblk * log_gamma)
    bc = lambda a: jnp.broadcast_to(a[:, :, None], (RET_HEADS, tblk, HALF_W))
    cdv = jnp.broadcast_to(cd[:, None, None], (RET_HEADS, 1, HEAD_W))
    return dmat, bc(qd), bc(kd), cdv


def retention(proj, cos, sin, tblk=256):
    b, s, _ = proj.shape
    tblk = _pick(s, tblk)
    dmat, qd, kd, cdv = _retention_tables(tblk)
    nh = RET_HEADS

    def col(off):
        return pl.BlockSpec((1, tblk, HEAD_W), lambda bi, h, t: (bi, t, off + h))

    def per_head(shape):
        return pl.BlockSpec((1,) + shape, lambda bi, h, t: (h, 0, 0))

    return pl.pallas_call(
        _retention_kernel,
        grid=(b, nh, s // tblk),
        in_specs=[col(0), col(nh), col(2 * nh), col(3 * nh),
                  pl.BlockSpec((tblk, HALF_W), lambda bi, h, t: (t, 0)),
                  pl.BlockSpec((tblk, HALF_W), lambda bi, h, t: (t, 0)),
                  per_head((tblk, tblk)), per_head((tblk, HALF_W)), per_head((tblk, HALF_W)),
                  per_head((1, HEAD_W))],
        out_specs=pl.BlockSpec((1, tblk, HEAD_W), lambda bi, h, t: (bi, t, h)),
        out_shape=jax.ShapeDtypeStruct((b, s, nh * HEAD_W), BF16),
        scratch_shapes=[pltpu.VMEM((HEAD_W, HEAD_W), F32)],
        compiler_params=_params(("parallel", "parallel", "arbitrary")),
        name="retention",
    )(proj, proj, proj, proj, cos, sin, dmat, qd, kd, cdv)


def _mem_attn_kernel(q_ref, kv_ref, o_ref):
    scale = HEAD_W ** -0.5
    width = MEM_HEADS * HEAD_W
    for h in range(MEM_HEADS):
        q = q_ref[0, :, h * HEAD_W:(h + 1) * HEAD_W]
        mk = kv_ref[0, :, h * HEAD_W:(h + 1) * HEAD_W]
        mv = kv_ref[0, :, width + h * HEAD_W:width + (h + 1) * HEAD_W]
        logits = lax.dot_general(q, mk, (((1,), (1,)), ((), ())), preferred_element_type=F32) * scale
        mx = jnp.max(logits, axis=-1, keepdims=True)
        p = jnp.exp(logits - mx)
        p = p / jnp.sum(p, axis=-1, keepdims=True)
        o = jnp.dot(p.astype(BF16), mv, preferred_element_type=F32)
        o_ref[0, :, h * HEAD_W:(h + 1) * HEAD_W] = o.astype(o_ref.dtype)


def mem_attention(proj, mkv, col_block, tq=512):
    b, s, _ = proj.shape
    mt = mkv.shape[1]
    width = MEM_HEADS * HEAD_W
    tq = _pick(s, tq)
    return pl.pallas_call(
        _mem_attn_kernel,
        grid=(b, s // tq),
        in_specs=[pl.BlockSpec((1, tq, width), lambda bi, t: (bi, t, col_block)),
                  pl.BlockSpec((1, mt, 2 * width), lambda bi, t: (bi, 0, 0))],
        out_specs=pl.BlockSpec((1, tq, width), lambda bi, t: (bi, t, 0)),
        out_shape=jax.ShapeDtypeStruct((b, s, width), BF16),
        compiler_params=_params(("parallel", "parallel")),
        name="mem_attention",
    )(proj, mkv)


DIFF_TILE = 512
LOG2E = math.log2(math.e)


def _t5_bucket(rel):
    half = NUM_BUCKETS // 2
    max_exact = half // 2
    ret = jnp.where(rel > 0, half, 0)
    n = jnp.abs(rel)
    nf = jnp.maximum(n, 1).astype(F32)
    large = max_exact + (jnp.log(nf / max_exact) / math.log(MAX_DISTANCE / max_exact)
                         * (half - max_exact)).astype(jnp.int32)
    large = jnp.minimum(large, half - 1)
    return ret + jnp.where(n < max_exact, n, large)


def _bias_buckets(t):
    i = jnp.arange(t)
    rel = i[:, None] - i[None, :]
    bucket = _t5_bucket(jnp.stack([rel, rel - t]))
    masked = (i[:, None] // CHUNK) > (i[None, :] // CHUNK)
    return bucket.at[0].set(jnp.where(masked, NUM_BUCKETS, bucket[0]))


def _bias_kernel(table_ref, bucket_ref, o_ref):
    h = pl.program_id(0)
    far = table_ref[NUM_BUCKETS // 2 - 1, h]
    rows = 64

    def body(c, carry):
        for slot in range(2):
            b = bucket_ref[slot, pl.ds(c * rows, rows), :]
            acc = jnp.full(b.shape, -jnp.inf, F32)
            for i in range(NUM_BUCKETS):
                acc = jnp.where(b == i, (table_ref[i, h] - far) * LOG2E, acc)
            o_ref[0, slot, pl.ds(c * rows, rows), :] = acc
        return carry

    lax.fori_loop(0, bucket_ref.shape[1] // rows, body, 0)


def bias_tiles(rel_bias):
    t = DIFF_TILE
    assert t >= MAX_DISTANCE and t % CHUNK == 0
    nh = rel_bias.shape[1]
    return pl.pallas_call(
        _bias_kernel,
        grid=(nh,),
        in_specs=[pl.BlockSpec(memory_space=pltpu.SMEM),
                  pl.BlockSpec((2, t, t), lambda h: (0, 0, 0))],
        out_specs=pl.BlockSpec((1, 2, t, t), lambda h: (h, 0, 0, 0)),
        out_shape=jax.ShapeDtypeStruct((nh, 2, t, t), F32),
        compiler_params=_params(("parallel",)),
        name="bias_tiles",
    )(rel_bias.astype(F32), _bias_buckets(t))


def _diff_attn_kernel(q_ref, k_ref, vt_ref, bias_ref, lq1_ref, lk1_ref, lq2_ref, lk2_ref, subln_ref,
                      o_ref, acc_ref, *, lam_init):
    t = DIFF_TILE
    i = pl.program_id(2)
    qt = (q_ref[0].astype(F32) * (HALF_W ** -0.5 * LOG2E)).T
    qts = (qt[:HALF_W].astype(BF16), qt[HALF_W:].astype(BF16))
    acc_ref[...] = jnp.zeros_like(acc_ref)

    def process(items, carry):
        scores = []
        for kb, _ in items:
            kc = k_ref[0, pl.ds(kb * t, t), :]
            scores.append([jnp.dot(kc[:, half * HALF_W:(half + 1) * HALF_W], qts[half],
                                   preferred_element_type=F32) for half in range(2)])
        m, l = [carry[0], carry[2]], [carry[1], carry[3]]
        for (kb, slot), s_pair in zip(items, scores):
            vt = vt_ref[0, 0, kb]
            for half in range(2):
                s = s_pair[half]
                if slot is not None:
                    s = s + bias_ref[0, slot]
                m_new = jnp.maximum(m[half], jnp.max(s, axis=0, keepdims=True))
                alpha = jnp.exp2(m[half] - m_new)
                p = jnp.exp2(s - m_new)
                l[half] = alpha * l[half] + jnp.sum(p, axis=0, keepdims=True)
                pv = jnp.dot(vt, p.astype(BF16), preferred_element_type=F32)
                acc_ref[half] = alpha * acc_ref[half] + pv
                m[half] = m_new
        return m[0], l[0], m[1], l[1]

    neg = jnp.full((1, t), -jnp.inf, F32)
    zero = jnp.zeros((1, t), F32)
    n_far = jnp.maximum(i - 1, 0)
    one, two = n_far % 2, (n_far // 2) % 2
    carry = lax.cond(one == 1, lambda c: process([(0, None)], c), lambda c: c, (neg, zero, neg, zero))
    carry = lax.cond(two == 1, lambda c: process([(one, None), (one + 1, None)], c), lambda c: c, carry)
    base = one + 2 * two
    carry = lax.fori_loop(
        0, n_far // 4, lambda j, c: process([(base + 4 * j + r, None) for r in range(4)], c), carry)
    m1, l1, m2, l2 = lax.cond(i >= 1, lambda c: process([(i - 1, 1), (i, 0)], c),
                              lambda c: process([(0, 0)], c), carry)

    lam = (jnp.exp(jnp.sum(lq1_ref[...] * lk1_ref[...])) - jnp.exp(jnp.sum(lq2_ref[...] * lk2_ref[...]))
           + lam_init)
    o = acc_ref[0] * (1.0 / l1) - acc_ref[1] * (lam / l2)
    o = o * lax.rsqrt(jnp.mean(o * o, axis=0, keepdims=True) + EPS) * subln_ref[...]
    o_ref[0] = (o * (1.0 - lam_init)).T.astype(o_ref.dtype)


def diff_attention(qproj, kv, bias, lq1, lk1, lq2, lk2, subln_g, lam_init):
    b, s, _ = qproj.shape
    nh = DIFF_HEADS
    t = DIFF_TILE
    assert s % t == 0
    nc = s // t
    vt = kv[:, :, nh * HEAD_W:].reshape(b, nc, t, nh, HEAD_W).transpose(0, 3, 1, 4, 2)
    vec = lambda a: a.astype(F32).reshape(1, -1)
    small = lambda n: pl.BlockSpec((1, n), lambda bi, h, qb: (0, 0))
    return pl.pallas_call(
        functools.partial(_diff_attn_kernel, lam_init=lam_init),
        grid=(b, nh, nc),
        in_specs=[pl.BlockSpec((1, t, HEAD_W), lambda bi, h, qb: (bi, qb, h)),
                  pl.BlockSpec((1, s, HEAD_W), lambda bi, h, qb: (bi, 0, h)),
                  pl.BlockSpec((1, 1, nc, HEAD_W, t), lambda bi, h, qb: (bi, h, 0, 0, 0)),
                  pl.BlockSpec((1, 2, t, t), lambda bi, h, qb: (h, 0, 0, 0)),
                  small(HALF_W), small(HALF_W), small(HALF_W), small(HALF_W),
                  pl.BlockSpec((HEAD_W, 1), lambda bi, h, qb: (0, 0))],
        out_specs=pl.BlockSpec((1, t, HEAD_W), lambda bi, h, qb: (bi, qb, h)),
        out_shape=jax.ShapeDtypeStruct((b, s, nh * HEAD_W), BF16),
        scratch_shapes=[pltpu.VMEM((2, HEAD_W, t), F32)],
        compiler_params=_params(("parallel", "parallel", "arbitrary")),
        name="diff_attention",
    )(qproj, kv, vt, bias, vec(lq1), vec(lk1), vec(lq2), vec(lk2), subln_g.astype(F32).reshape(-1, 1))


def _up_conv_gate_kernel(h_ref, ssq_ref, wa_ref, wg_ref, cwa_ref, cwg_ref, cba_ref, cbg_ref, o_ref,
                         halo_a, halo_g, *, tiles_per_seq, rc):
    first = (pl.program_id(1) % tiles_per_seq) == 0
    bm, d = h_ref.shape
    wa, wg = wa_ref[...], wg_ref[...]
    rstd = _row_rstd(ssq_ref[...], d)

    us = []
    for c in range(bm // rc):
        hc = h_ref[c * rc:(c + 1) * rc, :]
        rs = rstd[c * rc:(c + 1) * rc]
        us.append((jnp.dot(hc, wa, preferred_element_type=F32) * rs, jnp.dot(hc, wg, preferred_element_type=F32) * rs))

    row8 = lax.broadcasted_iota(jnp.int32, (8, wa.shape[1]), 0)

    def conv(u, prev, cw_ref, cb_ref):
        top = u[:8]
        t1 = jnp.where(row8 == 0, prev[7:8], pltpu.roll(top, 1, 0))
        t2 = jnp.where(row8 == 0, prev[6:7], jnp.where(row8 == 1, prev[7:8], pltpu.roll(top, 2, 0)))
        u1 = jnp.concatenate([t1, pltpu.roll(u, 1, 0)[8:]], axis=0)
        u2 = jnp.concatenate([t2, pltpu.roll(u, 2, 0)[8:]], axis=0)
        cw = cw_ref[...]
        return cb_ref[...] + cw[0:1] * u2 + cw[1:2] * u1 + cw[2:3] * u

    prev_a = jnp.where(first, 0.0, halo_a[...])
    prev_g = jnp.where(first, 0.0, halo_g[...])
    for c, (ua, ug) in enumerate(us):
        a = conv(ua, prev_a, cwa_ref, cba_ref)
        g = conv(ug, prev_g, cwg_ref, cbg_ref)
        o_ref[c * rc:(c + 1) * rc, :] = (a * jax.nn.sigmoid(a) * g).astype(o_ref.dtype)
        prev_a, prev_g = ua[rc - 8:], ug[rc - 8:]
    halo_a[...] = prev_a
    halo_g[...] = prev_g


def up_conv_gate(h, ssq, w_up, conv_w, conv_b, layer, seq, bm=1024, bn=512, rc=256):
    m, d = h.shape
    f = w_up.shape[2] // 2
    bm, bn = _pick(seq, bm), _pick(f, bn)
    nj = f // bn

    def cols(rows_, off):
        return pl.BlockSpec((None, rows_, bn), lambda j, i: (layer, 0, off + j))

    return pl.pallas_call(
        functools.partial(_up_conv_gate_kernel, tiles_per_seq=seq // bm, rc=_pick(bm, rc)),
        grid=(nj, m // bm),
        in_specs=[pl.BlockSpec((bm, d), lambda j, i: (i, 0)),
                  pl.BlockSpec((bm, LANES), lambda j, i: (i, 0)),
                  cols(d, 0), cols(d, nj), cols(3, 0), cols(3, nj), cols(1, 0), cols(1, nj)],
        out_specs=pl.BlockSpec((bm, bn), lambda j, i: (i, j)),
        out_shape=jax.ShapeDtypeStruct((m, f), BF16),
        scratch_shapes=[pltpu.VMEM((8, bn), F32), pltpu.VMEM((8, bn), F32)],
        compiler_params=_params(("parallel", "arbitrary")),
        name="up_conv_gate",
    )(h, ssq, w_up, w_up, conv_w, conv_w, conv_b, conv_b)


def kernel(x, mem, a_w_in, b_w_q, w_kv, w_out, mem_w_kv, ffn_w_up, ffn_conv_w, ffn_conv_b, ffn_w_down,
           attn_norm, ffn_norm, kv_norm, mem_norm, final_norm, rel_bias,
           lambda_q1, lambda_k1, lambda_q2, lambda_k2, diff_subln):
    b, s, d = x.shape
    depth = w_out.shape[0]
    n_a = a_w_in.shape[0]
    m = b * s
    mix_w = RET_HEADS * HEAD_W
    mem_w = MEM_HEADS * HEAD_W
    assert d == mix_w + mem_w and a_w_in.shape[2] == 4 * mix_w + mem_w and b_w_q.shape[2] == mix_w + mem_w

    pos = jnp.arange(s, dtype=F32)
    inv_freq = ROPE_BASE ** (-jnp.arange(HALF_W, dtype=F32) / HALF_W)
    ang = pos[:, None] * inv_freq[None, :]
    cos, sin = jnp.cos(ang), jnp.sin(ang)

    xf = x.reshape(m, d)
    memn = rmsnorm(mem.reshape(-1, d), mem_norm, BF16)
    w_up = ffn_w_up.astype(BF16)
    conv_b = ffn_conv_b[:, None, :]
    h, ssq = rmsnorm(xf, attn_norm[0], BF16), None
    hkv = kv = bias = None
    for layer in range(depth):
        if layer == n_a:
            kv = matmul([hkv], w_kv[None], 0, BF16, ssq=ssq).reshape(b, s, -1)
            bias = bias_tiles(rel_bias)
        mkv = matmul([memn], mem_w_kv, layer, BF16).reshape(b, -1, 2 * mem_w)
        if layer < n_a:
            proj = matmul([h], a_w_in, layer, BF16, ssq=ssq).reshape(b, s, -1)
            mix = retention(proj, cos, sin)
            mo = mem_attention(proj, mkv, (4 * mix_w) // mem_w)
        else:
            bl = layer - n_a
            lam_init = 0.8 - 0.6 * math.exp(-0.3 * layer)
            proj = matmul([h], b_w_q, bl, BF16, ssq=ssq).reshape(b, s, -1)
            mix = diff_attention(proj, kv, bias, lambda_q1[bl], lambda_k1[bl], lambda_q2[bl],
                                 lambda_k2[bl], diff_subln[bl], lam_init)
            mo = mem_attention(proj, mkv, mix_w // mem_w)
        xf, (h,), ssq = matmul([mix.reshape(m, mix_w), mo.reshape(m, mem_w)], w_out, layer, F32, resid=xf,
                               gains=[ffn_norm[layer]], bm=2048, bn=256, name="out_proj")
        act = up_conv_gate(h, ssq, w_up, ffn_conv_w, conv_b, layer, s)
        if layer + 1 == depth:
            xf = matmul([act], ffn_w_down, layer, F32, resid=xf, bm=1024, bn=256, name="down_proj")
        else:
            gains = [attn_norm[layer + 1]] + ([kv_norm] if layer + 1 == n_a else [])
            xf, hs, ssq = matmul([act], ffn_w_down, layer, F32, resid=xf, gains=gains,
                                 bm=1024, bn=256, name="down_proj")
            h, hkv = hs[0], hs[-1]
    return rmsnorm(xf, final_norm, F32).reshape(b, s, d)
```

```python
import functools
import math

import jax
import jax.numpy as jnp
from jax import lax
from jax.experimental import pallas as pl
from jax.experimental.pallas import tpu as pltpu

F32 = jnp.float32
BF16 = jnp.bfloat16

EPS = 1e-6
CHUNK = 64
ROPE_BASE = 10000.0
NUM_BUCKETS = 32
MAX_DISTANCE = 128
MEM_HEADS = 4
RET_HEADS = 12
DIFF_HEADS = 12
HEAD_W = 256
HALF_W = HEAD_W // 2

VMEM_LIMIT = 56 * 1024 * 1024


def _params(sem):
    return pltpu.CompilerParams(dimension_semantics=sem, vmem_limit_bytes=VMEM_LIMIT)


def _pick(n, pref):
    b = min(pref, n)
    while n % b:
        b //= 2
    return b


def _rmsnorm_kernel(x_ref, g_ref, o_ref):
    x = x_ref[...]
    ms = jnp.mean(x * x, axis=-1, keepdims=True)
    o_ref[...] = (x * lax.rsqrt(ms + EPS) * g_ref[...]).astype(o_ref.dtype)


def rmsnorm(x, g, out_dtype, bm=256):
    m, d = x.shape
    bm = _pick(m, bm)
    return pl.pallas_call(
        _rmsnorm_kernel,
        grid=(m // bm,),
        in_specs=[pl.BlockSpec((bm, d), lambda i: (i, 0)),
                  pl.BlockSpec((1, d), lambda i: (0, 0))],
        out_specs=pl.BlockSpec((bm, d), lambda i: (i, 0)),
        out_shape=jax.ShapeDtypeStruct((m, d), out_dtype),
        compiler_params=_params(("parallel",)),
        name="rmsnorm",
    )(x, g.reshape(1, d))


LANES = 128


def _row_rstd(ssq, norm_dim):
    return lax.rsqrt(jnp.sum(ssq, axis=1, keepdims=True) * (1.0 / norm_dim) + EPS)


def _lane_partial_ssq(x):
    sq = x * x
    part = sq[:, :LANES]
    for c in range(1, x.shape[1] // LANES):
        part = part + sq[:, c * LANES:(c + 1) * LANES]
    return part


def _mm_kernel(*refs, n_act, has_resid, has_rowscale, n_gain, norm_dim, rc):
    it = iter(refs)
    a_refs = [next(it) for _ in range(n_act)]
    w_refs = [next(it) for _ in range(n_act)]
    r_ref = next(it) if has_resid else None
    ssq_ref = next(it) if has_rowscale else None
    g_refs = [next(it) for _ in range(n_gain)]
    o_ref = next(it)
    xg_refs = [next(it) for _ in range(n_gain)]
    ssq_out = next(it) if n_gain else None

    bm, bn = o_ref.shape
    ws = [w_ref[...].astype(BF16) for w_ref in w_refs]
    accs = []
    for c in range(bm // rc):
        rows = slice(c * rc, (c + 1) * rc)
        acc = None
        for a_ref, w in zip(a_refs, ws):
            part = jnp.dot(a_ref[rows, :], w, preferred_element_type=F32)
            acc = part if acc is None else acc + part
        accs.append(acc)
    if n_gain:
        @pl.when(pl.program_id(1) == 0)
        def _():
            ssq_out[...] = jnp.zeros_like(ssq_out)

    for c, acc in enumerate(accs):
        rows = slice(c * rc, (c + 1) * rc)
        if has_rowscale:
            acc = acc * _row_rstd(ssq_ref[rows, :], norm_dim)
        if has_resid:
            acc = r_ref[rows, :] + acc
        o_ref[rows, :] = acc.astype(o_ref.dtype)
        for g_ref, xg_ref in zip(g_refs, xg_refs):
            xg_ref[rows, :] = (acc * g_ref[...]).astype(xg_ref.dtype)
        if n_gain:
            ssq_out[rows, :] += _lane_partial_ssq(acc)


def matmul(acts, w, layer, out_dtype, resid=None, ssq=None, gains=(), bm=2048, bn=512, name="matmul"):
    m = acts[0].shape[0]
    k, n = w.shape[1], w.shape[2]
    bm, bn = _pick(m, bm), _pick(n, bn)
    tile = pl.BlockSpec((bm, bn), lambda i, j: (i, j))
    rows = pl.BlockSpec((bm, LANES), lambda i, j: (i, 0))
    act_specs, w_specs, off = [], [], 0
    for a in acts:
        ki = a.shape[1]
        assert off % ki == 0
        act_specs.append(pl.BlockSpec((bm, ki), lambda i, j: (i, 0), pipeline_mode=pl.Buffered(1)))
        w_specs.append(pl.BlockSpec((None, ki, bn), lambda i, j, kb=off // ki: (layer, kb, j)))
        off += ki
    assert off == k
    in_specs = act_specs + w_specs
    args = list(acts) + [w] * len(acts)
    if resid is not None:
        in_specs.append(tile)
        args.append(resid)
    if ssq is not None:
        in_specs.append(rows)
        args.append(ssq)
    for g in gains:
        in_specs.append(pl.BlockSpec((1, bn), lambda i, j: (0, j)))
        args.append(g.reshape(1, n))
    out_specs = [tile] + [tile] * len(gains)
    out_shape = [jax.ShapeDtypeStruct((m, n), out_dtype)] + [jax.ShapeDtypeStruct((m, n), BF16)] * len(gains)
    if gains:
        out_specs.append(rows)
        out_shape.append(jax.ShapeDtypeStruct((m, LANES), F32))
    outs = pl.pallas_call(
        functools.partial(_mm_kernel, n_act=len(acts), has_resid=resid is not None,
                          has_rowscale=ssq is not None, n_gain=len(gains), norm_dim=k,
                          rc=bm),
        grid=(m // bm, n // bn),
        in_specs=in_specs,
        out_specs=out_specs,
        out_shape=out_shape,
        compiler_params=_params(("parallel", "arbitrary")),
        name=name,
    )(*args)
    if not gains:
        return outs[0]
    return outs[0], outs[1:-1], outs[-1]


def _retention_kernel(q_ref, k_ref, v_ref, g_ref, cos_ref, sin_ref, dmat_ref, qd_ref, kd_ref, cd_ref,
                      o_ref, state_ref, *, tblk):
    @pl.when(pl.program_id(2) == 0)
    def _():
        state_ref[...] = jnp.zeros_like(state_ref)

    kscale = HEAD_W ** -0.5
    qd = qd_ref[0]
    kd = kd_ref[0]
    state = state_ref[...]
    for sub in range(q_ref.shape[1] // tblk):
        rows = slice(sub * tblk, (sub + 1) * tblk)
        cos = cos_ref[rows, :]
        sin = sin_ref[rows, :]

        def rot(x):
            x1, x2 = x[:, :HALF_W], x[:, HALF_W:]
            return x1 * cos - x2 * sin, x2 * cos + x1 * sin

        q1, q2 = rot(q_ref[0, rows, :].astype(F32))
        k1, k2 = rot(k_ref[0, rows, :].astype(F32))
        k1, k2 = k1 * kscale, k2 * kscale
        v = v_ref[0, rows, :]

        qr = jnp.concatenate([q1, q2], axis=-1).astype(BF16)
        kr = jnp.concatenate([k1, k2], axis=-1).astype(BF16)
        qdec = jnp.concatenate([q1 * qd, q2 * qd], axis=-1).astype(BF16)
        kdec = jnp.concatenate([k1 * kd, k2 * kd], axis=-1)

        scores = lax.dot_general(qr, kr, (((1,), (1,)), ((), ())), preferred_element_type=F32)
        scores = (scores * dmat_ref[0]).astype(BF16)
        o = jnp.dot(scores, v, preferred_element_type=F32)
        o = o + jnp.dot(qdec, state.astype(BF16), preferred_element_type=F32)
        kv = jnp.dot(kdec.T.astype(BF16), v, preferred_element_type=F32)
        state = state * cd_ref[0] + kv

        mu = jnp.mean(o, axis=-1, keepdims=True)
        oc = o - mu
        on = oc * lax.rsqrt(jnp.mean(oc * oc, axis=-1, keepdims=True) + EPS)
        g = g_ref[0, rows, :].astype(F32)
        o_ref[0, rows, :] = (g * jax.nn.sigmoid(g) * on).astype(o_ref.dtype)
    state_ref[...] = state


def _retention_tables(tblk):
    h = jnp.arange(RET_HEADS, dtype=F32)
    log_gamma = jnp.log(1.0 - 2.0 ** (-5.0 - h))
    idx = jnp.arange(tblk, dtype=F32)
    ci = jnp.arange(tblk) // CHUNK
    diff = idx[:, None] - idx[None, :]
    same = ci[:, None] == ci[None, :]
    earlier = ci[None, :] < ci[:, None]
    expo = jnp.where(same, jnp.abs(diff), diff)
    dmat = jnp.where((same | earlier)[None], jnp.exp(log_gamma[:, None, None] * expo[None]), 0.0)
    qd = jnp.exp((idx[None, :] + 1.0) * log_gamma[:, None])
    kd = jnp.exp((tblk - 1.0 - idx[None, :]) * log_gamma[:, None])
    cd = jnp.exp(tblk * log_gamma)
    bc = lambda a: jnp.broadcast_to(a[:, :, None], (RET_HEADS, tblk, HALF_W))
    cdv = jnp.broadcast_to(cd[:, None, None], (RET_HEADS, 1, HEAD_W))
    return dmat, bc(qd), bc(kd), cdv


def retention(proj, cos, sin, tblk=256, nsub=4):
    b, s, _ = proj.shape
    tblk = _pick(s, tblk)
    assert tblk % CHUNK == 0
    step = _pick(s, tblk * nsub)
    dmat, qd, kd, cdv = _retention_tables(tblk)
    nh = RET_HEADS

    def col(off):
        return pl.BlockSpec((1, step, HEAD_W), lambda bi, h, t: (bi, t, off + h))

    def per_head(shape):
        return pl.BlockSpec((1,) + shape, lambda bi, h, t: (h, 0, 0))

    return pl.pallas_call(
        functools.partial(_retention_kernel, tblk=tblk),
        grid=(b, nh, s // step),
        in_specs=[col(0), col(nh), col(2 * nh), col(3 * nh),
                  pl.BlockSpec((step, HALF_W), lambda bi, h, t: (t, 0)),
                  pl.BlockSpec((step, HALF_W), lambda bi, h, t: (t, 0)),
                  per_head((tblk, tblk)), per_head((tblk, HALF_W)), per_head((tblk, HALF_W)),
                  per_head((1, HEAD_W))],
        out_specs=pl.BlockSpec((1, step, HEAD_W), lambda bi, h, t: (bi, t, h)),
        out_shape=jax.ShapeDtypeStruct((b, s, nh * HEAD_W), BF16),
        scratch_shapes=[pltpu.VMEM((HEAD_W, HEAD_W), F32)],
        compiler_params=_params(("parallel", "parallel", "arbitrary")),
        name="retention",
    )(proj, proj, proj, proj, cos, sin, dmat, qd, kd, cdv)


def _mem_attn_kernel(q_ref, kv_ref, o_ref):
    scale = HEAD_W ** -0.5
    width = MEM_HEADS * HEAD_W
    for h in range(MEM_HEADS):
        q = q_ref[0, :, h * HEAD_W:(h + 1) * HEAD_W]
        mk = kv_ref[0, :, h * HEAD_W:(h + 1) * HEAD_W]
        mv = kv_ref[0, :, width + h * HEAD_W:width + (h + 1) * HEAD_W]
        logits = lax.dot_general(q, mk, (((1,), (1,)), ((), ())), preferred_element_type=F32) * scale
        mx = jnp.max(logits, axis=-1, keepdims=True)
        p = jnp.exp(logits - mx)
        p = p / jnp.sum(p, axis=-1, keepdims=True)
        o = jnp.dot(p.astype(BF16), mv, preferred_element_type=F32)
        o_ref[0, :, h * HEAD_W:(h + 1) * HEAD_W] = o.astype(o_ref.dtype)


def mem_attention(proj, mkv, col_block, tq=512):
    b, s, _ = proj.shape
    mt = mkv.shape[1]
    width = MEM_HEADS * HEAD_W
    tq = _pick(s, tq)
    return pl.pallas_call(
        _mem_attn_kernel,
        grid=(b, s // tq),
        in_specs=[pl.BlockSpec((1, tq, width), lambda bi, t: (bi, t, col_block)),
                  pl.BlockSpec((1, mt, 2 * width), lambda bi, t: (bi, 0, 0))],
        out_specs=pl.BlockSpec((1, tq, width), lambda bi, t: (bi, t, 0)),
        out_shape=jax.ShapeDtypeStruct((b, s, width), BF16),
        compiler_params=_params(("parallel", "parallel")),
        name="mem_attention",
    )(proj, mkv)


DIFF_TILE = 512
LOG2E = math.log2(math.e)


def _t5_bucket(rel):
    half = NUM_BUCKETS // 2
    max_exact = half // 2
    ret = jnp.where(rel > 0, half, 0)
    n = jnp.abs(rel)
    nf = jnp.maximum(n, 1).astype(F32)
    large = max_exact + (jnp.log(nf / max_exact) / math.log(MAX_DISTANCE / max_exact)
                         * (half - max_exact)).astype(jnp.int32)
    large = jnp.minimum(large, half - 1)
    return ret + jnp.where(n < max_exact, n, large)


def _bias_buckets(t):
    i = jnp.arange(t)
    rel = i[:, None] - i[None, :]
    bucket = _t5_bucket(jnp.stack([rel, rel - t]))
    masked = (i[:, None] // CHUNK) > (i[None, :] // CHUNK)
    return bucket.at[0].set(jnp.where(masked, NUM_BUCKETS, bucket[0]))


def _bias_kernel(table_ref, bucket_ref, o_ref):
    h = pl.program_id(0)
    far = table_ref[NUM_BUCKETS // 2 - 1, h]
    rows = 64

    def body(c, carry):
        for slot in range(2):
            b = bucket_ref[slot, pl.ds(c * rows, rows), :]
            acc = jnp.full(b.shape, -jnp.inf, F32)
            for i in range(NUM_BUCKETS):
                acc = jnp.where(b == i, (table_ref[i, h] - far) * LOG2E, acc)
            o_ref[0, slot, pl.ds(c * rows, rows), :] = acc
        return carry

    lax.fori_loop(0, bucket_ref.shape[1] // rows, body, 0)


def bias_tiles(rel_bias):
    t = DIFF_TILE
    assert t >= MAX_DISTANCE and t % CHUNK == 0
    nh = rel_bias.shape[1]
    return pl.pallas_call(
        _bias_kernel,
        grid=(nh,),
        in_specs=[pl.BlockSpec(memory_space=pltpu.SMEM),
                  pl.BlockSpec((2, t, t), lambda h: (0, 0, 0))],
        out_specs=pl.BlockSpec((1, 2, t, t), lambda h: (h, 0, 0, 0)),
        out_shape=jax.ShapeDtypeStruct((nh, 2, t, t), F32),
        compiler_params=_params(("parallel",)),
        name="bias_tiles",
    )(rel_bias.astype(F32), _bias_buckets(t))


V_ROWS = HEAD_W + 16
STALE_MAX_SLACK = 64.0


def _diff_attn_kernel(q_ref, k_ref, vt_ref, bias_ref, lq1_ref, lk1_ref, lq2_ref, lk2_ref, subln_ref,
                      o_ref, acc_ref, *, lam_init):
    t = DIFF_TILE
    i = pl.program_id(2)
    qt = (q_ref[0].astype(F32) * (HALF_W ** -0.5 * LOG2E)).T
    qts = (qt[:HALF_W].astype(BF16), qt[HALF_W:].astype(BF16))
    acc_ref[...] = jnp.zeros_like(acc_ref)

    def all_scores(items):
        out = []
        for kb, _ in items:
            kc = k_ref[0, pl.ds(kb * t, t), :]
            out.append([jnp.dot(kc[:, half * HALF_W:(half + 1) * HALF_W], qts[half],
                                preferred_element_type=F32) for half in range(2)])
        return out

    def process(items, m):
        m = list(m)
        for (kb, slot), s_pair in zip(items, all_scores(items)):
            vt = vt_ref[0, 0, kb]
            for half in range(2):
                s = s_pair[half]
                if slot is not None:
                    s = s + bias_ref[0, slot]
                m_new = jnp.maximum(m[half], jnp.max(s, axis=0, keepdims=True))
                p = jnp.exp2(s - m_new)
                pv = jnp.dot(vt, p.astype(BF16), preferred_element_type=F32)
                acc_ref[half] = jnp.exp2(m[half] - m_new) * acc_ref[half] + pv
                m[half] = m_new
        return tuple(m)

    def process_stale(items, m):
        pvs, cms = [None, None], [None, None]
        for (kb, _), s_pair in zip(items, all_scores(items)):
            vt = vt_ref[0, 0, kb]
            for half in range(2):
                s = s_pair[half]
                cm = jnp.max(s, axis=0, keepdims=True)
                pv = jnp.dot(vt, jnp.exp2(s - m[half]).astype(BF16), preferred_element_type=F32)
                cms[half] = cm if cms[half] is None else jnp.maximum(cms[half], cm)
                pvs[half] = pv if pvs[half] is None else pvs[half] + pv
        excess = jnp.maximum(jnp.max(cms[0] - m[0]), jnp.max(cms[1] - m[1]))

        def commit(ops):
            m_old, cms_, pvs_ = ops
            out = []
            for half in range(2):
                m_new = jnp.maximum(m_old[half], cms_[half])
                acc_ref[half] = (acc_ref[half] + pvs_[half]) * jnp.exp2(m_old[half] - m_new)
                out.append(m_new)
            return tuple(out)

        return lax.cond(jnp.logical_not(excess <= STALE_MAX_SLACK),
                        lambda ops: process(items, ops[0]), commit, (tuple(m), tuple(cms), tuple(pvs)))

    neg = jnp.full((1, t), -jnp.inf, F32)
    n_far = jnp.maximum(i - 1, 0)
    one, two = n_far % 2, (n_far // 2) % 2
    m = lax.cond(one == 1, lambda c: process([(0, None)], c), lambda c: c, (neg, neg))
    m = lax.cond(two == 1, lambda c: process([(one, None), (one + 1, None)], c), lambda c: c, m)
    base = one + 2 * two
    m = lax.fori_loop(
        0, n_far // 4, lambda j, c: process_stale([(base + 4 * j + r, None) for r in range(4)], c), m)
    lax.cond(i >= 1, lambda c: process([(i - 1, 1), (i, 0)], c), lambda c: process([(0, 0)], c), m)

    lam = (jnp.exp(jnp.sum(lq1_ref[...] * lk1_ref[...])) - jnp.exp(jnp.sum(lq2_ref[...] * lk2_ref[...]))
           + lam_init)
    l1 = acc_ref[0, HEAD_W:HEAD_W + 1, :]
    l2 = acc_ref[1, HEAD_W:HEAD_W + 1, :]
    o = acc_ref[0, :HEAD_W, :] * (1.0 / l1) - acc_ref[1, :HEAD_W, :] * (lam / l2)
    o = o * lax.rsqrt(jnp.mean(o * o, axis=0, keepdims=True) + EPS) * subln_ref[...]
    o_ref[0] = (o * (1.0 - lam_init)).T.astype(o_ref.dtype)


def diff_attention(qproj, kv, bias, lq1, lk1, lq2, lk2, subln_g, lam_init):
    b, s, _ = qproj.shape
    nh = DIFF_HEADS
    t = DIFF_TILE
    assert s % t == 0
    nc = s // t
    vt = kv[:, :, nh * HEAD_W:].reshape(b, nc, t, nh, HEAD_W).transpose(0, 3, 1, 4, 2)
    vt = jnp.concatenate([vt, jnp.ones((b, nh, nc, V_ROWS - HEAD_W, t), vt.dtype)], axis=3)
    vec = lambda a: a.astype(F32).reshape(1, -1)
    small = lambda n: pl.BlockSpec((1, n), lambda bi, h, qb: (0, 0))
    return pl.pallas_call(
        functools.partial(_diff_attn_kernel, lam_init=lam_init),
        grid=(b, nh, nc),
        in_specs=[pl.BlockSpec((1, t, HEAD_W), lambda bi, h, qb: (bi, qb, h)),
                  pl.BlockSpec((1, s, HEAD_W), lambda bi, h, qb: (bi, 0, h)),
                  pl.BlockSpec((1, 1, nc, V_ROWS, t), lambda bi, h, qb: (bi, h, 0, 0, 0)),
                  pl.BlockSpec((1, 2, t, t), lambda bi, h, qb: (h, 0, 0, 0)),
                  small(HALF_W), small(HALF_W), small(HALF_W), small(HALF_W),
                  pl.BlockSpec((HEAD_W, 1), lambda bi, h, qb: (0, 0))],
        out_specs=pl.BlockSpec((1, t, HEAD_W), lambda bi, h, qb: (bi, qb, h)),
        out_shape=jax.ShapeDtypeStruct((b, s, nh * HEAD_W), BF16),
        scratch_shapes=[pltpu.VMEM((2, V_ROWS, t), F32)],
        compiler_params=_params(("parallel", "parallel", "arbitrary")),
        name="diff_attention",
    )(qproj, kv, vt, bias, vec(lq1), vec(lk1), vec(lq2), vec(lk2), subln_g.astype(F32).reshape(-1, 1))


def _up_conv_gate_kernel(h_ref, ssq_ref, wa_ref, wg_ref, cwa_ref, cwg_ref, cba_ref, cbg_ref, o_ref,
                         halo_a, halo_g, *, tiles_per_seq, rc):
    first = (pl.program_id(1) % tiles_per_seq) == 0
    bm, d = h_ref.shape
    wa, wg = wa_ref[...], wg_ref[...]
    rstd = _row_rstd(ssq_ref[...], d)

    us = []
    for c in range(bm // rc):
        hc = h_ref[c * rc:(c + 1) * rc, :]
        rs = rstd[c * rc:(c + 1) * rc]
        us.append((jnp.dot(hc, wa, preferred_element_type=F32) * rs, jnp.dot(hc, wg, preferred_element_type=F32) * rs))

    row8 = lax.broadcasted_iota(jnp.int32, (8, wa.shape[1]), 0)

    def conv(u, prev, cw_ref, cb_ref):
        top = u[:8]
        t1 = jnp.where(row8 == 0, prev[7:8], pltpu.roll(top, 1, 0))
        t2 = jnp.where(row8 == 0, prev[6:7], jnp.where(row8 == 1, prev[7:8], pltpu.roll(top, 2, 0)))
        u1 = jnp.concatenate([t1, pltpu.roll(u, 1, 0)[8:]], axis=0)
        u2 = jnp.concatenate([t2, pltpu.roll(u, 2, 0)[8:]], axis=0)
        cw = cw_ref[...]
        return cb_ref[...] + cw[0:1] * u2 + cw[1:2] * u1 + cw[2:3] * u

    prev_a = jnp.where(first, 0.0, halo_a[...])
    prev_g = jnp.where(first, 0.0, halo_g[...])
    for c, (ua, ug) in enumerate(us):
        a = conv(ua, prev_a, cwa_ref, cba_ref)
        g = conv(ug, prev_g, cwg_ref, cbg_ref)
        o_ref[c * rc:(c + 1) * rc, :] = (a * jax.nn.sigmoid(a) * g).astype(o_ref.dtype)
        prev_a, prev_g = ua[rc - 8:], ug[rc - 8:]
    halo_a[...] = prev_a
    halo_g[...] = prev_g


def up_conv_gate(h, ssq, w_up, conv_w, conv_b, layer, seq, bm=1024, bn=512, rc=256):
    m, d = h.shape
    f = w_up.shape[2] // 2
    bm, bn = _pick(seq, bm), _pick(f, bn)
    nj = f // bn

    def cols(rows_, off):
        return pl.BlockSpec((None, rows_, bn), lambda j, i: (layer, 0, off + j))

    return pl.pallas_call(
        functools.partial(_up_conv_gate_kernel, tiles_per_seq=seq // bm, rc=_pick(bm, rc)),
        grid=(nj, m // bm),
        in_specs=[pl.BlockSpec((bm, d), lambda j, i: (i, 0)),
                  pl.BlockSpec((bm, LANES), lambda j, i: (i, 0)),
                  cols(d, 0), cols(d, nj), cols(3, 0), cols(3, nj), cols(1, 0), cols(1, nj)],
        out_specs=pl.BlockSpec((bm, bn), lambda j, i: (i, j)),
        out_shape=jax.ShapeDtypeStruct((m, f), BF16),
        scratch_shapes=[pltpu.VMEM((8, bn), F32), pltpu.VMEM((8, bn), F32)],
        compiler_params=_params(("parallel", "arbitrary")),
        name="up_conv_gate",
    )(h, ssq, w_up, w_up, conv_w, conv_w, conv_b, conv_b)


def kernel(x, mem, a_w_in, b_w_q, w_kv, w_out, mem_w_kv, ffn_w_up, ffn_conv_w, ffn_conv_b, ffn_w_down,
           attn_norm, ffn_norm, kv_norm, mem_norm, final_norm, rel_bias,
           lambda_q1, lambda_k1, lambda_q2, lambda_k2, diff_subln):
    b, s, d = x.shape
    depth = w_out.shape[0]
    n_a = a_w_in.shape[0]
    m = b * s
    mix_w = RET_HEADS * HEAD_W
    mem_w = MEM_HEADS * HEAD_W
    assert d == mix_w + mem_w and a_w_in.shape[2] == 4 * mix_w + mem_w and b_w_q.shape[2] == mix_w + mem_w

    pos = jnp.arange(s, dtype=F32)
    inv_freq = ROPE_BASE ** (-jnp.arange(HALF_W, dtype=F32) / HALF_W)
    ang = pos[:, None] * inv_freq[None, :]
    cos, sin = jnp.cos(ang), jnp.sin(ang)

    xf = x.reshape(m, d)
    memn = rmsnorm(mem.reshape(-1, d), mem_norm, BF16)
    w_up = ffn_w_up.astype(BF16)
    w_down = ffn_w_down.astype(BF16)
    conv_b = ffn_conv_b[:, None, :]
    h, ssq = rmsnorm(xf, attn_norm[0], BF16), None
    hkv = kv = bias = None
    for layer in range(depth):
        if layer == n_a:
            kv = matmul([hkv], w_kv[None], 0, BF16, ssq=ssq).reshape(b, s, -1)
            bias = bias_tiles(rel_bias)
        mkv = matmul([memn], mem_w_kv, layer, BF16).reshape(b, -1, 2 * mem_w)
        if layer < n_a:
            proj = matmul([h], a_w_in, layer, BF16, ssq=ssq).reshape(b, s, -1)
            mix = retention(proj, cos, sin)
            mo = mem_attention(proj, mkv, (4 * mix_w) // mem_w)
        else:
            bl = layer - n_a
            lam_init = 0.8 - 0.6 * math.exp(-0.3 * layer)
            proj = matmul([h], b_w_q, bl, BF16, ssq=ssq).reshape(b, s, -1)
            mix = diff_attention(proj, kv, bias, lambda_q1[bl], lambda_k1[bl], lambda_q2[bl],
                                 lambda_k2[bl], diff_subln[bl], lam_init)
            mo = mem_attention(proj, mkv, mix_w // mem_w)
        xf, (h,), ssq = matmul([mix.reshape(m, mix_w), mo.reshape(m, mem_w)], w_out, layer, F32, resid=xf,
                               gains=[ffn_norm[layer]], bm=2048, bn=256, name="out_proj")
        act = up_conv_gate(h, ssq, w_up, ffn_conv_w, conv_b, layer, s)
        if layer + 1 == depth:
            xf = matmul([act], w_down, layer, F32, resid=xf, bm=1024, bn=512, name="down_proj")
        else:
            gains = [attn_norm[layer + 1]] + ([kv_norm] if layer + 1 == n_a else [])
            xf, hs, ssq = matmul([act], w_down, layer, F32, resid=xf, gains=gains,
                                 bm=1024, bn=512, name="down_proj")
            h, hkv = hs[0], hs[-1]
    return rmsnorm(xf, final_norm, F32).reshape(b, s, d)
```

```python
import functools
import math

import jax
import jax.numpy as jnp
from jax import lax
from jax.experimental import pallas as pl
from jax.experimental.pallas import tpu as pltpu

F32 = jnp.float32
BF16 = jnp.bfloat16

EPS = 1e-6
CHUNK = 64
ROPE_BASE = 10000.0
NUM_BUCKETS = 32
MAX_DISTANCE = 128
MEM_HEADS = 4
RET_HEADS = 12
DIFF_HEADS = 12
HEAD_W = 256
HALF_W = HEAD_W // 2

VMEM_LIMIT = 56 * 1024 * 1024


def _params(sem):
    return pltpu.CompilerParams(dimension_semantics=sem, vmem_limit_bytes=VMEM_LIMIT)


def _pick(n, pref):
    b = min(pref, n)
    while n % b:
        b //= 2
    return b


def _rmsnorm_kernel(x_ref, g_ref, o_ref):
    x = x_ref[...]
    ms = jnp.mean(x * x, axis=-1, keepdims=True)
    o_ref[...] = (x * lax.rsqrt(ms + EPS) * g_ref[...]).astype(o_ref.dtype)


def rmsnorm(x, g, out_dtype, bm=256):
    m, d = x.shape
    bm = _pick(m, bm)
    return pl.pallas_call(
        _rmsnorm_kernel,
        grid=(m // bm,),
        in_specs=[pl.BlockSpec((bm, d), lambda i: (i, 0)),
                  pl.BlockSpec((1, d), lambda i: (0, 0))],
        out_specs=pl.BlockSpec((bm, d), lambda i: (i, 0)),
        out_shape=jax.ShapeDtypeStruct((m, d), out_dtype),
        compiler_params=_params(("parallel",)),
        name="rmsnorm",
    )(x, g.reshape(1, d))


LANES = 128


def _row_rstd(ssq, norm_dim):
    return lax.rsqrt(jnp.sum(ssq, axis=1, keepdims=True) * (1.0 / norm_dim) + EPS)


def _lane_partial_ssq(x):
    sq = x * x
    part = sq[:, :LANES]
    for c in range(1, x.shape[1] // LANES):
        part = part + sq[:, c * LANES:(c + 1) * LANES]
    return part


def _mm_kernel(*refs, n_act, has_resid, has_rowscale, n_gain, norm_dim, rc):
    it = iter(refs)
    a_refs = [next(it) for _ in range(n_act)]
    w_refs = [next(it) for _ in range(n_act)]
    r_ref = next(it) if has_resid else None
    ssq_ref = next(it) if has_rowscale else None
    g_refs = [next(it) for _ in range(n_gain)]
    o_ref = next(it)
    xg_refs = [next(it) for _ in range(n_gain)]
    ssq_out = next(it) if n_gain else None

    bm, bn = o_ref.shape
    ws = [w_ref[...].astype(BF16) for w_ref in w_refs]
    accs = []
    for c in range(bm // rc):
        rows = slice(c * rc, (c + 1) * rc)
        acc = None
        for a_ref, w in zip(a_refs, ws):
            part = jnp.dot(a_ref[rows, :], w, preferred_element_type=F32)
            acc = part if acc is None else acc + part
        accs.append(acc)
    if n_gain:
        @pl.when(pl.program_id(1) == 0)
        def _():
            ssq_out[...] = jnp.zeros_like(ssq_out)

    for c, acc in enumerate(accs):
        rows = slice(c * rc, (c + 1) * rc)
        if has_rowscale:
            acc = acc * _row_rstd(ssq_ref[rows, :], norm_dim)
        if has_resid:
            acc = r_ref[rows, :] + acc
        o_ref[rows, :] = acc.astype(o_ref.dtype)
        for g_ref, xg_ref in zip(g_refs, xg_refs):
            xg_ref[rows, :] = (acc * g_ref[...]).astype(xg_ref.dtype)
        if n_gain:
            ssq_out[rows, :] += _lane_partial_ssq(acc)


def matmul(acts, w, layer, out_dtype, resid=None, ssq=None, gains=(), n_cols=None, bm=2048, bn=512, name="matmul"):
    m = acts[0].shape[0]
    k, n = w.shape[1], n_cols or w.shape[2]
    bm, bn = _pick(m, bm), _pick(n, bn)
    tile = pl.BlockSpec((bm, bn), lambda i, j: (i, j))
    rows = pl.BlockSpec((bm, LANES), lambda i, j: (i, 0))
    act_specs, w_specs, off = [], [], 0
    for a in acts:
        ki = a.shape[1]
        assert off % ki == 0
        act_specs.append(pl.BlockSpec((bm, ki), lambda i, j: (i, 0), pipeline_mode=pl.Buffered(1)))
        w_specs.append(pl.BlockSpec((None, ki, bn), lambda i, j, kb=off // ki: (layer, kb, j)))
        off += ki
    assert off == k
    in_specs = act_specs + w_specs
    args = list(acts) + [w] * len(acts)
    if resid is not None:
        in_specs.append(tile)
        args.append(resid)
    if ssq is not None:
        in_specs.append(rows)
        args.append(ssq)
    for g in gains:
        in_specs.append(pl.BlockSpec((1, bn), lambda i, j: (0, j)))
        args.append(g.reshape(1, n))
    out_specs = [tile] + [tile] * len(gains)
    out_shape = [jax.ShapeDtypeStruct((m, n), out_dtype)] + [jax.ShapeDtypeStruct((m, n), BF16)] * len(gains)
    if gains:
        out_specs.append(rows)
        out_shape.append(jax.ShapeDtypeStruct((m, LANES), F32))
    outs = pl.pallas_call(
        functools.partial(_mm_kernel, n_act=len(acts), has_resid=resid is not None,
                          has_rowscale=ssq is not None, n_gain=len(gains), norm_dim=k,
                          rc=bm),
        grid=(m // bm, n // bn),
        in_specs=in_specs,
        out_specs=out_specs,
        out_shape=out_shape,
        compiler_params=_params(("parallel", "arbitrary")),
        name=name,
    )(*args)
    if not gains:
        return outs[0]
    return outs[0], outs[1:-1], outs[-1]


def _retention_kernel(q_ref, k_ref, v_ref, g_ref, cos_ref, sin_ref, dmat_ref, qd_ref, kd_ref, cd_ref,
                      o_ref, state_ref, *, tblk):
    @pl.when(pl.program_id(2) == 0)
    def _():
        state_ref[...] = jnp.zeros_like(state_ref)

    kscale = HEAD_W ** -0.5
    qd = qd_ref[0]
    kd = kd_ref[0]
    state = state_ref[...]
    for sub in range(q_ref.shape[1] // tblk):
        rows = slice(sub * tblk, (sub + 1) * tblk)
        cos = cos_ref[rows, :]
        sin = sin_ref[rows, :]

        def rot(x):
            x1, x2 = x[:, :HALF_W], x[:, HALF_W:]
            return x1 * cos - x2 * sin, x2 * cos + x1 * sin

        q1, q2 = rot(q_ref[0, rows, :].astype(F32))
        k1, k2 = rot(k_ref[0, rows, :].astype(F32))
        k1, k2 = k1 * kscale, k2 * kscale
        v = v_ref[0, rows, :]

        qr = jnp.concatenate([q1, q2], axis=-1).astype(BF16)
        kr = jnp.concatenate([k1, k2], axis=-1).astype(BF16)
        qdec = jnp.concatenate([q1 * qd, q2 * qd], axis=-1).astype(BF16)
        kdec = jnp.concatenate([k1 * kd, k2 * kd], axis=-1)

        scores = lax.dot_general(qr, kr, (((1,), (1,)), ((), ())), preferred_element_type=F32)
        scores = (scores * dmat_ref[0]).astype(BF16)
        o = jnp.dot(scores, v, preferred_element_type=F32)
        o = o + jnp.dot(qdec, state.astype(BF16), preferred_element_type=F32)
        kv = jnp.dot(kdec.T.astype(BF16), v, preferred_element_type=F32)
        state = state * cd_ref[0] + kv

        mu = jnp.mean(o, axis=-1, keepdims=True)
        oc = o - mu
        on = oc * lax.rsqrt(jnp.mean(oc * oc, axis=-1, keepdims=True) + EPS)
        g = g_ref[0, rows, :].astype(F32)
        o_ref[0, rows, :] = (g * jax.nn.sigmoid(g) * on).astype(o_ref.dtype)
    state_ref[...] = state


def _retention_tables(tblk):
    h = jnp.arange(RET_HEADS, dtype=F32)
    log_gamma = jnp.log(1.0 - 2.0 ** (-5.0 - h))
    idx = jnp.arange(tblk, dtype=F32)
    ci = jnp.arange(tblk) // CHUNK
    diff = idx[:, None] - idx[None, :]
    same = ci[:, None] == ci[None, :]
    earlier = ci[None, :] < ci[:, None]
    expo = jnp.where(same, jnp.abs(diff), diff)
    dmat = jnp.where((same | earlier)[None], jnp.exp(log_gamma[:, None, None] * expo[None]), 0.0)
    qd = jnp.exp((idx[None, :] + 1.0) * log_gamma[:, None])
    kd = jnp.exp((tblk - 1.0 - idx[None, :]) * log_gamma[:, None])
    cd = jnp.exp(tblk * log_gamma)
    bc = lambda a: jnp.broadcast_to(a[:, :, None], (RET_HEADS, tblk, HALF_W))
    cdv = jnp.broadcast_to(cd[:, None, None], (RET_HEADS, 1, HEAD_W))
    return dmat, bc(qd), bc(kd), cdv


def retention(proj, cos, sin, tblk=256, nsub=4):
    b, s, _ = proj.shape
    tblk = _pick(s, tblk)
    assert tblk % CHUNK == 0
    step = _pick(s, tblk * nsub)
    dmat, qd, kd, cdv = _retention_tables(tblk)
    nh = RET_HEADS

    def col(off):
        return pl.BlockSpec((1, step, HEAD_W), lambda bi, h, t: (bi, t, off + h))

    def per_head(shape):
        return pl.BlockSpec((1,) + shape, lambda bi, h, t: (h, 0, 0))

    return pl.pallas_call(
        functools.partial(_retention_kernel, tblk=tblk),
        grid=(b, nh, s // step),
        in_specs=[col(0), col(nh), col(2 * nh), col(3 * nh),
                  pl.BlockSpec((step, HALF_W), lambda bi, h, t: (t, 0)),
                  pl.BlockSpec((step, HALF_W), lambda bi, h, t: (t, 0)),
                  per_head((tblk, tblk)), per_head((tblk, HALF_W)), per_head((tblk, HALF_W)),
                  per_head((1, HEAD_W))],
        out_specs=pl.BlockSpec((1, step, HEAD_W), lambda bi, h, t: (bi, t, h)),
        out_shape=jax.ShapeDtypeStruct((b, s, nh * HEAD_W), BF16),
        scratch_shapes=[pltpu.VMEM((HEAD_W, HEAD_W), F32)],
        compiler_params=_params(("parallel", "parallel", "arbitrary")),
        name="retention",
    )(proj, proj, proj, proj, cos, sin, dmat, qd, kd, cdv)


def _mem_attn_kernel(q_ref, kv_ref, o_ref):
    scale = HEAD_W ** -0.5
    width = MEM_HEADS * HEAD_W
    for h in range(MEM_HEADS):
        q = q_ref[0, :, h * HEAD_W:(h + 1) * HEAD_W]
        mk = kv_ref[0, :, h * HEAD_W:(h + 1) * HEAD_W]
        mv = kv_ref[0, :, width + h * HEAD_W:width + (h + 1) * HEAD_W]
        logits = lax.dot_general(q, mk, (((1,), (1,)), ((), ())), preferred_element_type=F32) * scale
        mx = jnp.max(logits, axis=-1, keepdims=True)
        p = jnp.exp(logits - mx)
        p = p / jnp.sum(p, axis=-1, keepdims=True)
        o = jnp.dot(p.astype(BF16), mv, preferred_element_type=F32)
        o_ref[0, :, h * HEAD_W:(h + 1) * HEAD_W] = o.astype(o_ref.dtype)


def mem_attention(proj, mkv, col_block, tq=512):
    b, s, _ = proj.shape
    mt = mkv.shape[1]
    width = MEM_HEADS * HEAD_W
    tq = _pick(s, tq)
    return pl.pallas_call(
        _mem_attn_kernel,
        grid=(b, s // tq),
        in_specs=[pl.BlockSpec((1, tq, width), lambda bi, t: (bi, t, col_block)),
                  pl.BlockSpec((1, mt, 2 * width), lambda bi, t: (bi, 0, 0))],
        out_specs=pl.BlockSpec((1, tq, width), lambda bi, t: (bi, t, 0)),
        out_shape=jax.ShapeDtypeStruct((b, s, width), BF16),
        compiler_params=_params(("parallel", "parallel")),
        name="mem_attention",
    )(proj, mkv)


DIFF_TILE = 512
LOG2E = math.log2(math.e)


def _t5_bucket(rel):
    half = NUM_BUCKETS // 2
    max_exact = half // 2
    ret = jnp.where(rel > 0, half, 0)
    n = jnp.abs(rel)
    nf = jnp.maximum(n, 1).astype(F32)
    large = max_exact + (jnp.log(nf / max_exact) / math.log(MAX_DISTANCE / max_exact)
                         * (half - max_exact)).astype(jnp.int32)
    large = jnp.minimum(large, half - 1)
    return ret + jnp.where(n < max_exact, n, large)


def _bias_buckets(t):
    i = jnp.arange(t)
    rel = i[:, None] - i[None, :]
    bucket = _t5_bucket(jnp.stack([rel, rel - t]))
    masked = (i[:, None] // CHUNK) > (i[None, :] // CHUNK)
    return bucket.at[0].set(jnp.where(masked, NUM_BUCKETS, bucket[0]))


def _bias_kernel(table_ref, bucket_ref, o_ref):
    h = pl.program_id(0)
    far = table_ref[NUM_BUCKETS // 2 - 1, h]
    rows = 64

    def body(c, carry):
        for slot in range(2):
            b = bucket_ref[slot, pl.ds(c * rows, rows), :]
            acc = jnp.full(b.shape, -jnp.inf, F32)
            for i in range(NUM_BUCKETS):
                acc = jnp.where(b == i, (table_ref[i, h] - far) * LOG2E, acc)
            o_ref[0, slot, pl.ds(c * rows, rows), :] = acc
        return carry

    lax.fori_loop(0, bucket_ref.shape[1] // rows, body, 0)


def bias_tiles(rel_bias):
    t = DIFF_TILE
    assert t >= MAX_DISTANCE and t % CHUNK == 0
    nh = rel_bias.shape[1]
    return pl.pallas_call(
        _bias_kernel,
        grid=(nh,),
        in_specs=[pl.BlockSpec(memory_space=pltpu.SMEM),
                  pl.BlockSpec((2, t, t), lambda h: (0, 0, 0))],
        out_specs=pl.BlockSpec((1, 2, t, t), lambda h: (h, 0, 0, 0)),
        out_shape=jax.ShapeDtypeStruct((nh, 2, t, t), F32),
        compiler_params=_params(("parallel",)),
        name="bias_tiles",
    )(rel_bias.astype(F32), _bias_buckets(t))


V_ROWS = HEAD_W + 16
STALE_MAX_SLACK = 64.0


def _diff_attn_kernel(q_ref, k_ref, vt_ref, bias_ref, lq1_ref, lk1_ref, lq2_ref, lk2_ref, subln_ref,
                      o_ref, acc_ref, *, lam_init):
    t = DIFF_TILE
    i = pl.program_id(2)
    qt = (q_ref[0].astype(F32) * (HALF_W ** -0.5 * LOG2E)).T
    qts = (qt[:HALF_W].astype(BF16), qt[HALF_W:].astype(BF16))
    acc_ref[...] = jnp.zeros_like(acc_ref)

    def all_scores(items):
        out = []
        for kb, _ in items:
            kc = k_ref[0, pl.ds(kb * t, t), :]
            out.append([jnp.dot(kc[:, half * HALF_W:(half + 1) * HALF_W], qts[half],
                                preferred_element_type=F32) for half in range(2)])
        return out

    def process(items, m):
        m = list(m)
        for (kb, slot), s_pair in zip(items, all_scores(items)):
            vt = vt_ref[0, 0, kb]
            for half in range(2):
                s = s_pair[half]
                if slot is not None:
                    s = s + bias_ref[0, slot]
                m_new = jnp.maximum(m[half], jnp.max(s, axis=0, keepdims=True))
                p = jnp.exp2(s - m_new)
                pv = jnp.dot(vt, p.astype(BF16), preferred_element_type=F32)
                acc_ref[half] = jnp.exp2(m[half] - m_new) * acc_ref[half] + pv
                m[half] = m_new
        return tuple(m)

    def process_stale(items, m):
        pvs, cms = [None, None], [None, None]
        for (kb, slot), s_pair in zip(items, all_scores(items)):
            vt = vt_ref[0, 0, kb]
            for half in range(2):
                s = s_pair[half]
                if slot is not None:
                    s = s + bias_ref[0, slot]
                cm = jnp.max(s, axis=0, keepdims=True)
                pv = jnp.dot(vt, jnp.exp2(s - m[half]).astype(BF16), preferred_element_type=F32)
                cms[half] = cm if cms[half] is None else jnp.maximum(cms[half], cm)
                pvs[half] = pv if pvs[half] is None else pvs[half] + pv
        excess = jnp.maximum(jnp.max(cms[0] - m[0]), jnp.max(cms[1] - m[1]))

        def commit(ops):
            m_old, cms_, pvs_ = ops
            out = []
            for half in range(2):
                m_new = jnp.maximum(m_old[half], cms_[half])
                acc_ref[half] = (acc_ref[half] + pvs_[half]) * jnp.exp2(m_old[half] - m_new)
                out.append(m_new)
            return tuple(out)

        return lax.cond(jnp.logical_not(excess <= STALE_MAX_SLACK),
                        lambda ops: process(items, ops[0]), commit, (tuple(m), tuple(cms), tuple(pvs)))

    neg = jnp.full((1, t), -jnp.inf, F32)
    n_far = jnp.maximum(i - 1, 0)
    one, two = n_far % 2, (n_far // 2) % 2
    m = lax.cond(one == 1, lambda c: process([(0, None)], c), lambda c: c, (neg, neg))
    m = lax.cond(two == 1, lambda c: process([(one, None), (one + 1, None)], c), lambda c: c, m)
    base = one + 2 * two
    m = lax.fori_loop(
        0, n_far // 4, lambda j, c: process_stale([(base + 4 * j + r, None) for r in range(4)], c), m)
    lax.cond(i >= 1, lambda c: process_stale([(i - 1, 1), (i, 0)], c), lambda c: process([(0, 0)], c), m)

    lam = (jnp.exp(jnp.sum(lq1_ref[...] * lk1_ref[...])) - jnp.exp(jnp.sum(lq2_ref[...] * lk2_ref[...]))
           + lam_init)
    l1 = acc_ref[0, HEAD_W:HEAD_W + 1, :]
    l2 = acc_ref[1, HEAD_W:HEAD_W + 1, :]
    o = acc_ref[0, :HEAD_W, :] * (1.0 / l1) - acc_ref[1, :HEAD_W, :] * (lam / l2)
    o = o * lax.rsqrt(jnp.mean(o * o, axis=0, keepdims=True) + EPS) * subln_ref[...]
    o_ref[0] = (o * (1.0 - lam_init)).T.astype(o_ref.dtype)


def _value_proj_t_kernel(a_ref, w_ref, ssq_ref, o_ref, *, norm_dim):
    acc = jnp.dot(a_ref[...], w_ref[...].astype(BF16), preferred_element_type=F32)
    acc = acc * _row_rstd(ssq_ref[...], norm_dim)
    _, heads, chunks, _, t = o_ref.shape
    ones = jnp.ones((V_ROWS - HEAD_W, t), o_ref.dtype)
    for hh in range(heads):
        for c in range(chunks):
            blk = acc[c * t:(c + 1) * t, hh * HEAD_W:(hh + 1) * HEAD_W]
            o_ref[0, hh, c, :HEAD_W, :] = blk.T.astype(o_ref.dtype)
            o_ref[0, hh, c, HEAD_W:, :] = ones


def value_proj_t(h, ssq, w, col0, batch, seq, bm=2048, bn=512):
    m, k = h.shape
    t = DIFF_TILE
    nh = (w.shape[1] - col0) // HEAD_W
    bm = _pick(seq, bm)
    assert bm % t == 0 and bn % HEAD_W == 0 and col0 % bn == 0 and nh * HEAD_W % bn == 0
    tiles_per_seq = seq // bm
    return pl.pallas_call(
        functools.partial(_value_proj_t_kernel, norm_dim=k),
        grid=(m // bm, nh * HEAD_W // bn),
        in_specs=[pl.BlockSpec((bm, k), lambda i, j: (i, 0), pipeline_mode=pl.Buffered(1)),
                  pl.BlockSpec((k, bn), lambda i, j: (0, col0 // bn + j)),
                  pl.BlockSpec((bm, LANES), lambda i, j: (i, 0))],
        out_specs=pl.BlockSpec((1, bn // HEAD_W, bm // t, V_ROWS, t),
                               lambda i, j: (i // tiles_per_seq, j, i % tiles_per_seq, 0, 0)),
        out_shape=jax.ShapeDtypeStruct((batch, nh, seq // t, V_ROWS, t), BF16),
        compiler_params=_params(("parallel", "arbitrary")),
        name="value_proj_t",
    )(h, w, ssq)


def diff_attention(qproj, kproj, vt, bias, lq1, lk1, lq2, lk2, subln_g, lam_init):
    b, s, _ = qproj.shape
    nh = DIFF_HEADS
    t = DIFF_TILE
    assert s % t == 0
    nc = s // t
    vec = lambda a: a.astype(F32).reshape(1, -1)
    small = lambda n: pl.BlockSpec((1, n), lambda bi, h, qb: (0, 0))
    return pl.pallas_call(
        functools.partial(_diff_attn_kernel, lam_init=lam_init),
        grid=(b, nh, nc),
        in_specs=[pl.BlockSpec((1, t, HEAD_W), lambda bi, h, qb: (bi, qb, h)),
                  pl.BlockSpec((1, s, HEAD_W), lambda bi, h, qb: (bi, 0, h)),
                  pl.BlockSpec((1, 1, nc, V_ROWS, t), lambda bi, h, qb: (bi, h, 0, 0, 0)),
                  pl.BlockSpec((1, 2, t, t), lambda bi, h, qb: (h, 0, 0, 0)),
                  small(HALF_W), small(HALF_W), small(HALF_W), small(HALF_W),
                  pl.BlockSpec((HEAD_W, 1), lambda bi, h, qb: (0, 0))],
        out_specs=pl.BlockSpec((1, t, HEAD_W), lambda bi, h, qb: (bi, qb, h)),
        out_shape=jax.ShapeDtypeStruct((b, s, nh * HEAD_W), BF16),
        scratch_shapes=[pltpu.VMEM((2, V_ROWS, t), F32)],
        compiler_params=_params(("parallel", "parallel", "arbitrary")),
        name="diff_attention",
    )(qproj, kproj, vt, bias, vec(lq1), vec(lk1), vec(lq2), vec(lk2), subln_g.astype(F32).reshape(-1, 1))


def _up_conv_gate_kernel(h_ref, ssq_ref, wa_ref, wg_ref, cwa_ref, cwg_ref, cba_ref, cbg_ref, o_ref,
                         halo_a, halo_g, *, tiles_per_seq, rc):
    first = (pl.program_id(0) % tiles_per_seq) == 0
    j = pl.program_id(1)
    bm, d = h_ref.shape
    wa, wg = wa_ref[...].astype(BF16), wg_ref[...].astype(BF16)
    rstd = _row_rstd(ssq_ref[...], d)

    us = []
    for c in range(bm // rc):
        hc = h_ref[c * rc:(c + 1) * rc, :]
        rs = rstd[c * rc:(c + 1) * rc]
        us.append((jnp.dot(hc, wa, preferred_element_type=F32) * rs, jnp.dot(hc, wg, preferred_element_type=F32) * rs))

    row8 = lax.broadcasted_iota(jnp.int32, (8, wa.shape[1]), 0)

    def conv(u, prev, cw_ref, cb_ref):
        top = u[:8]
        t1 = jnp.where(row8 == 0, prev[7:8], pltpu.roll(top, 1, 0))
        t2 = jnp.where(row8 == 0, prev[6:7], jnp.where(row8 == 1, prev[7:8], pltpu.roll(top, 2, 0)))
        u1 = jnp.concatenate([t1, pltpu.roll(u, 1, 0)[8:]], axis=0)
        u2 = jnp.concatenate([t2, pltpu.roll(u, 2, 0)[8:]], axis=0)
        cw = cw_ref[...]
        return cb_ref[...] + cw[0:1] * u2 + cw[1:2] * u1 + cw[2:3] * u

    prev_a = jnp.where(first, 0.0, halo_a[j])
    prev_g = jnp.where(first, 0.0, halo_g[j])
    for c, (ua, ug) in enumerate(us):
        a = conv(ua, prev_a, cwa_ref, cba_ref)
        g = conv(ug, prev_g, cwg_ref, cbg_ref)
        o_ref[c * rc:(c + 1) * rc, :] = (a * jax.nn.sigmoid(a) * g).astype(o_ref.dtype)
        prev_a, prev_g = ua[rc - 8:], ug[rc - 8:]
    halo_a[j] = prev_a
    halo_g[j] = prev_g


def up_conv_gate(h, ssq, w_up, conv_w, conv_b, layer, seq, bm=2048, bn=512, rc=256):
    m, d = h.shape
    f = w_up.shape[2] // 2
    bm, bn = _pick(seq, bm), _pick(f, bn)
    nj = f // bn

    def cols(rows_, off):
        return pl.BlockSpec((None, rows_, bn), lambda i, j: (layer, 0, off + j))

    return pl.pallas_call(
        functools.partial(_up_conv_gate_kernel, tiles_per_seq=seq // bm, rc=_pick(bm, rc)),
        grid=(m // bm, nj),
        in_specs=[pl.BlockSpec((bm, d), lambda i, j: (i, 0), pipeline_mode=pl.Buffered(1)),
                  pl.BlockSpec((bm, LANES), lambda i, j: (i, 0)),
                  cols(d, 0), cols(d, nj), cols(3, 0), cols(3, nj), cols(1, 0), cols(1, nj)],
        out_specs=pl.BlockSpec((bm, bn), lambda i, j: (i, j)),
        out_shape=jax.ShapeDtypeStruct((m, f), BF16),
        scratch_shapes=[pltpu.VMEM((nj, 8, bn), F32), pltpu.VMEM((nj, 8, bn), F32)],
        compiler_params=_params(("arbitrary", "arbitrary")),
        name="up_conv_gate",
    )(h, ssq, w_up, w_up, conv_w, conv_w, conv_b, conv_b)


def kernel(x, mem, a_w_in, b_w_q, w_kv, w_out, mem_w_kv, ffn_w_up, ffn_conv_w, ffn_conv_b, ffn_w_down,
           attn_norm, ffn_norm, kv_norm, mem_norm, final_norm, rel_bias,
           lambda_q1, lambda_k1, lambda_q2, lambda_k2, diff_subln):
    b, s, d = x.shape
    depth = w_out.shape[0]
    n_a = a_w_in.shape[0]
    m = b * s
    mix_w = RET_HEADS * HEAD_W
    mem_w = MEM_HEADS * HEAD_W
    assert d == mix_w + mem_w and a_w_in.shape[2] == 4 * mix_w + mem_w and b_w_q.shape[2] == mix_w + mem_w

    pos = jnp.arange(s, dtype=F32)
    inv_freq = ROPE_BASE ** (-jnp.arange(HALF_W, dtype=F32) / HALF_W)
    ang = pos[:, None] * inv_freq[None, :]
    cos, sin = jnp.cos(ang), jnp.sin(ang)

    xf = x.reshape(m, d)
    memn = rmsnorm(mem.reshape(-1, d), mem_norm, BF16)
    w_up = ffn_w_up.astype(BF16)
    w_down = ffn_w_down.astype(BF16)
    conv_b = ffn_conv_b[:, None, :]
    h, ssq = rmsnorm(xf, attn_norm[0], BF16), None
    hkv = kproj = vt = bias = None
    for layer in range(depth):
        if layer == n_a:
            kproj = matmul([hkv], w_kv[None], 0, BF16, ssq=ssq, n_cols=mix_w, name="key_proj").reshape(b, s, -1)
            vt = value_proj_t(hkv, ssq, w_kv, mix_w, b, s)
            bias = bias_tiles(rel_bias)
        mkv = matmul([memn], mem_w_kv, layer, BF16).reshape(b, -1, 2 * mem_w)
        if layer < n_a:
            proj = matmul([h], a_w_in, layer, BF16, ssq=ssq).reshape(b, s, -1)
            mix = retention(proj, cos, sin)
            mo = mem_attention(proj, mkv, (4 * mix_w) // mem_w)
        else:
            bl = layer - n_a
            lam_init = 0.8 - 0.6 * math.exp(-0.3 * layer)
            proj = matmul([h], b_w_q, bl, BF16, ssq=ssq).reshape(b, s, -1)
            mix = diff_attention(proj, kproj, vt, bias, lambda_q1[bl], lambda_k1[bl], lambda_q2[bl],
                                 lambda_k2[bl], diff_subln[bl], lam_init)
            mo = mem_attention(proj, mkv, mix_w // mem_w)
        xf, (h,), ssq = matmul([mix.reshape(m, mix_w), mo.reshape(m, mem_w)], w_out, layer, F32, resid=xf,
                               gains=[ffn_norm[layer]], bm=2048, bn=256, name="out_proj")
        act = up_conv_gate(h, ssq, w_up, ffn_conv_w, conv_b, layer, s)
        if layer + 1 == depth:
            xf = matmul([act], w_down, layer, F32, resid=xf, bm=1024, bn=512, name="down_proj")
        else:
            gains = [attn_norm[layer + 1]] + ([kv_norm] if layer + 1 == n_a else [])
            xf, hs, ssq = matmul([act], w_down, layer, F32, resid=xf, gains=gains,
                                 bm=1024, bn=512, name="down_proj")
            h, hkv = hs[0], hs[-1]
    return rmsnorm(xf, final_norm, F32).reshape(b, s, d)
```

```python
import functools
import math

import jax
import jax.numpy as jnp
from jax import lax
from jax.experimental import pallas as pl
from jax.experimental.pallas import tpu as pltpu

F32 = jnp.float32
BF16 = jnp.bfloat16

EPS = 1e-6
CHUNK = 64
ROPE_BASE = 10000.0
NUM_BUCKETS = 32
MAX_DISTANCE = 128
MEM_HEADS = 4
RET_HEADS = 12
DIFF_HEADS = 12
HEAD_W = 256
HALF_W = HEAD_W // 2

VMEM_LIMIT = 56 * 1024 * 1024


def _params(sem):
    return pltpu.CompilerParams(dimension_semantics=sem, vmem_limit_bytes=VMEM_LIMIT)


def _pick(n, pref):
    b = min(pref, n)
    while n % b:
        b //= 2
    return b


def _rmsnorm_kernel(x_ref, g_ref, o_ref):
    x = x_ref[...]
    ms = jnp.mean(x * x, axis=-1, keepdims=True)
    o_ref[...] = (x * lax.rsqrt(ms + EPS) * g_ref[...]).astype(o_ref.dtype)


def rmsnorm(x, g, out_dtype, bm=256):
    m, d = x.shape
    bm = _pick(m, bm)
    return pl.pallas_call(
        _rmsnorm_kernel,
        grid=(m // bm,),
        in_specs=[pl.BlockSpec((bm, d), lambda i: (i, 0)),
                  pl.BlockSpec((1, d), lambda i: (0, 0))],
        out_specs=pl.BlockSpec((bm, d), lambda i: (i, 0)),
        out_shape=jax.ShapeDtypeStruct((m, d), out_dtype),
        compiler_params=_params(("parallel",)),
        name="rmsnorm",
    )(x, g.reshape(1, d))


LANES = 128


def _row_rstd(ssq, norm_dim):
    return lax.rsqrt(jnp.sum(ssq, axis=1, keepdims=True) * (1.0 / norm_dim) + EPS)


def _lane_partial_ssq(x):
    sq = x * x
    part = sq[:, :LANES]
    for c in range(1, x.shape[1] // LANES):
        part = part + sq[:, c * LANES:(c + 1) * LANES]
    return part


def _mm_kernel(*refs, n_act, has_resid, has_rowscale, n_gain, norm_dim, rc):
    it = iter(refs)
    a_refs = [next(it) for _ in range(n_act)]
    w_refs = [next(it) for _ in range(n_act)]
    r_ref = next(it) if has_resid else None
    ssq_ref = next(it) if has_rowscale else None
    g_refs = [next(it) for _ in range(n_gain)]
    o_ref = next(it)
    xg_refs = [next(it) for _ in range(n_gain)]
    ssq_out = next(it) if n_gain else None

    bm, bn = o_ref.shape
    ws = [w_ref[...].astype(BF16) for w_ref in w_refs]
    accs = []
    for c in range(bm // rc):
        rows = slice(c * rc, (c + 1) * rc)
        acc = None
        for a_ref, w in zip(a_refs, ws):
            part = jnp.dot(a_ref[rows, :], w, preferred_element_type=F32)
            acc = part if acc is None else acc + part
        accs.append(acc)
    if n_gain:
        @pl.when(pl.program_id(1) == 0)
        def _():
            ssq_out[...] = jnp.zeros_like(ssq_out)

    for c, acc in enumerate(accs):
        rows = slice(c * rc, (c + 1) * rc)
        if has_rowscale:
            acc = acc * _row_rstd(ssq_ref[rows, :], norm_dim)
        if has_resid:
            acc = r_ref[rows, :] + acc
        o_ref[rows, :] = acc.astype(o_ref.dtype)
        for g_ref, xg_ref in zip(g_refs, xg_refs):
            xg_ref[rows, :] = (acc * g_ref[...]).astype(xg_ref.dtype)
        if n_gain:
            ssq_out[rows, :] += _lane_partial_ssq(acc)


def matmul(acts, w, layer, out_dtype, resid=None, ssq=None, gains=(), n_cols=None, bm=2048, bn=512, name="matmul"):
    m = acts[0].shape[0]
    k, n = w.shape[1], n_cols or w.shape[2]
    bm, bn = _pick(m, bm), _pick(n, bn)
    tile = pl.BlockSpec((bm, bn), lambda i, j: (i, j))
    rows = pl.BlockSpec((bm, LANES), lambda i, j: (i, 0))
    act_specs, w_specs, off = [], [], 0
    for a in acts:
        ki = a.shape[1]
        assert off % ki == 0
        act_specs.append(pl.BlockSpec((bm, ki), lambda i, j: (i, 0), pipeline_mode=pl.Buffered(1)))
        w_specs.append(pl.BlockSpec((None, ki, bn), lambda i, j, kb=off // ki: (layer, kb, j)))
        off += ki
    assert off == k
    in_specs = act_specs + w_specs
    args = list(acts) + [w] * len(acts)
    if resid is not None:
        in_specs.append(tile)
        args.append(resid)
    if ssq is not None:
        in_specs.append(rows)
        args.append(ssq)
    for g in gains:
        in_specs.append(pl.BlockSpec((1, bn), lambda i, j: (0, j)))
        args.append(g.reshape(1, n))
    out_specs = [tile] + [tile] * len(gains)
    out_shape = [jax.ShapeDtypeStruct((m, n), out_dtype)] + [jax.ShapeDtypeStruct((m, n), BF16)] * len(gains)
    if gains:
        out_specs.append(rows)
        out_shape.append(jax.ShapeDtypeStruct((m, LANES), F32))
    outs = pl.pallas_call(
        functools.partial(_mm_kernel, n_act=len(acts), has_resid=resid is not None,
                          has_rowscale=ssq is not None, n_gain=len(gains), norm_dim=k,
                          rc=bm),
        grid=(m // bm, n // bn),
        in_specs=in_specs,
        out_specs=out_specs,
        out_shape=out_shape,
        compiler_params=_params(("parallel", "arbitrary")),
        name=name,
    )(*args)
    if not gains:
        return outs[0]
    return outs[0], outs[1:-1], outs[-1]


def _retention_kernel(q_ref, k_ref, v_ref, g_ref, cos_ref, sin_ref, dmat_ref, qd_ref, kd_ref, cd_ref,
                      o_ref, state_ref, *, tblk):
    @pl.when(pl.program_id(2) == 0)
    def _():
        state_ref[...] = jnp.zeros_like(state_ref)

    kscale = HEAD_W ** -0.5
    qd = qd_ref[0]
    kd = kd_ref[0]
    state = state_ref[...]
    for sub in range(q_ref.shape[1] // tblk):
        rows = slice(sub * tblk, (sub + 1) * tblk)
        cos = cos_ref[rows, :]
        sin = sin_ref[rows, :]

        def rot(x):
            x1, x2 = x[:, :HALF_W], x[:, HALF_W:]
            return x1 * cos - x2 * sin, x2 * cos + x1 * sin

        q1, q2 = rot(q_ref[0, rows, :].astype(F32))
        k1, k2 = rot(k_ref[0, rows, :].astype(F32))
        k1, k2 = k1 * kscale, k2 * kscale
        v = v_ref[0, rows, :]

        qr = jnp.concatenate([q1, q2], axis=-1).astype(BF16)
        kr = jnp.concatenate([k1, k2], axis=-1).astype(BF16)
        qdec = jnp.concatenate([q1 * qd, q2 * qd], axis=-1).astype(BF16)
        kdec = jnp.concatenate([k1 * kd, k2 * kd], axis=-1)

        scores = lax.dot_general(qr, kr, (((1,), (1,)), ((), ())), preferred_element_type=F32)
        scores = (scores * dmat_ref[0]).astype(BF16)
        o = jnp.dot(scores, v, preferred_element_type=F32)
        o = o + jnp.dot(qdec, state.astype(BF16), preferred_element_type=F32)
        kv = jnp.dot(kdec.T.astype(BF16), v, preferred_element_type=F32)
        state = state * cd_ref[0] + kv

        mu = jnp.mean(o, axis=-1, keepdims=True)
        oc = o - mu
        on = oc * lax.rsqrt(jnp.mean(oc * oc, axis=-1, keepdims=True) + EPS)
        g = g_ref[0, rows, :].astype(F32)
        o_ref[0, rows, :] = (g * jax.nn.sigmoid(g) * on).astype(o_ref.dtype)
    state_ref[...] = state


def _retention_tables(tblk):
    h = jnp.arange(RET_HEADS, dtype=F32)
    log_gamma = jnp.log(1.0 - 2.0 ** (-5.0 - h))
    idx = jnp.arange(tblk, dtype=F32)
    ci = jnp.arange(tblk) // CHUNK
    diff = idx[:, None] - idx[None, :]
    same = ci[:, None] == ci[None, :]
    earlier = ci[None, :] < ci[:, None]
    expo = jnp.where(same, jnp.abs(diff), diff)
    dmat = jnp.where((same | earlier)[None], jnp.exp(log_gamma[:, None, None] * expo[None]), 0.0)
    qd = jnp.exp((idx[None, :] + 1.0) * log_gamma[:, None])
    kd = jnp.exp((tblk - 1.0 - idx[None, :]) * log_gamma[:, None])
    cd = jnp.exp(tblk * log_gamma)
    bc = lambda a: jnp.broadcast_to(a[:, :, None], (RET_HEADS, tblk, HALF_W))
    cdv = jnp.broadcast_to(cd[:, None, None], (RET_HEADS, 1, HEAD_W))
    return dmat, bc(qd), bc(kd), cdv


def retention(proj, cos, sin, tblk=256, nsub=4):
    b, s, _ = proj.shape
    tblk = _pick(s, tblk)
    assert tblk % CHUNK == 0
    step = _pick(s, tblk * nsub)
    dmat, qd, kd, cdv = _retention_tables(tblk)
    nh = RET_HEADS

    def col(off):
        return pl.BlockSpec((1, step, HEAD_W), lambda bi, h, t: (bi, t, off + h))

    def per_head(shape):
        return pl.BlockSpec((1,) + shape, lambda bi, h, t: (h, 0, 0))

    return pl.pallas_call(
        functools.partial(_retention_kernel, tblk=tblk),
        grid=(b, nh, s // step),
        in_specs=[col(0), col(nh), col(2 * nh), col(3 * nh),
                  pl.BlockSpec((step, HALF_W), lambda bi, h, t: (t, 0)),
                  pl.BlockSpec((step, HALF_W), lambda bi, h, t: (t, 0)),
                  per_head((tblk, tblk)), per_head((tblk, HALF_W)), per_head((tblk, HALF_W)),
                  per_head((1, HEAD_W))],
        out_specs=pl.BlockSpec((1, step, HEAD_W), lambda bi, h, t: (bi, t, h)),
        out_shape=jax.ShapeDtypeStruct((b, s, nh * HEAD_W), BF16),
        scratch_shapes=[pltpu.VMEM((HEAD_W, HEAD_W), F32)],
        compiler_params=_params(("parallel", "parallel", "arbitrary")),
        name="retention",
    )(proj, proj, proj, proj, cos, sin, dmat, qd, kd, cdv)


def _mem_attn_kernel(q_ref, kv_ref, o_ref):
    scale = HEAD_W ** -0.5
    width = MEM_HEADS * HEAD_W
    for h in range(MEM_HEADS):
        q = q_ref[0, :, h * HEAD_W:(h + 1) * HEAD_W]
        mk = kv_ref[0, :, h * HEAD_W:(h + 1) * HEAD_W]
        mv = kv_ref[0, :, width + h * HEAD_W:width + (h + 1) * HEAD_W]
        logits = lax.dot_general(q, mk, (((1,), (1,)), ((), ())), preferred_element_type=F32) * scale
        mx = jnp.max(logits, axis=-1, keepdims=True)
        p = jnp.exp(logits - mx)
        p = p / jnp.sum(p, axis=-1, keepdims=True)
        o = jnp.dot(p.astype(BF16), mv, preferred_element_type=F32)
        o_ref[0, :, h * HEAD_W:(h + 1) * HEAD_W] = o.astype(o_ref.dtype)


def mem_attention(proj, mkv, col_block, tq=512):
    b, s, _ = proj.shape
    mt = mkv.shape[1]
    width = MEM_HEADS * HEAD_W
    tq = _pick(s, tq)
    return pl.pallas_call(
        _mem_attn_kernel,
        grid=(b, s // tq),
        in_specs=[pl.BlockSpec((1, tq, width), lambda bi, t: (bi, t, col_block)),
                  pl.BlockSpec((1, mt, 2 * width), lambda bi, t: (bi, 0, 0))],
        out_specs=pl.BlockSpec((1, tq, width), lambda bi, t: (bi, t, 0)),
        out_shape=jax.ShapeDtypeStruct((b, s, width), BF16),
        compiler_params=_params(("parallel", "parallel")),
        name="mem_attention",
    )(proj, mkv)


DIFF_TILE = 512
LOG2E = math.log2(math.e)


def _t5_bucket(rel):
    half = NUM_BUCKETS // 2
    max_exact = half // 2
    ret = jnp.where(rel > 0, half, 0)
    n = jnp.abs(rel)
    nf = jnp.maximum(n, 1).astype(F32)
    large = max_exact + (jnp.log(nf / max_exact) / math.log(MAX_DISTANCE / max_exact)
                         * (half - max_exact)).astype(jnp.int32)
    large = jnp.minimum(large, half - 1)
    return ret + jnp.where(n < max_exact, n, large)


def _bias_buckets(t):
    i = jnp.arange(t)
    rel = i[:, None] - i[None, :]
    bucket = _t5_bucket(jnp.stack([rel, rel - t]))
    masked = (i[:, None] // CHUNK) > (i[None, :] // CHUNK)
    return bucket.at[0].set(jnp.where(masked, NUM_BUCKETS, bucket[0]))


def _bias_kernel(table_ref, bucket_ref, o_ref):
    h = pl.program_id(0)
    far = table_ref[NUM_BUCKETS // 2 - 1, h]
    rows = 64

    def body(c, carry):
        for slot in range(2):
            b = bucket_ref[slot, pl.ds(c * rows, rows), :]
            acc = jnp.full(b.shape, -jnp.inf, F32)
            for i in range(NUM_BUCKETS):
                acc = jnp.where(b == i, (table_ref[i, h] - far) * LOG2E, acc)
            o_ref[0, slot, pl.ds(c * rows, rows), :] = acc
        return carry

    lax.fori_loop(0, bucket_ref.shape[1] // rows, body, 0)


def bias_tiles(rel_bias):
    t = DIFF_TILE
    assert t >= MAX_DISTANCE and t % CHUNK == 0
    nh = rel_bias.shape[1]
    return pl.pallas_call(
        _bias_kernel,
        grid=(nh,),
        in_specs=[pl.BlockSpec(memory_space=pltpu.SMEM),
                  pl.BlockSpec((2, t, t), lambda h: (0, 0, 0))],
        out_specs=pl.BlockSpec((1, 2, t, t), lambda h: (h, 0, 0, 0)),
        out_shape=jax.ShapeDtypeStruct((nh, 2, t, t), F32),
        compiler_params=_params(("parallel",)),
        name="bias_tiles",
    )(rel_bias.astype(F32), _bias_buckets(t))


V_ROWS = HEAD_W + 16
STALE_MAX_SLACK = 64.0


def _diff_attn_kernel(q_ref, k_ref, vt_ref, bias_ref, lq1_ref, lk1_ref, lq2_ref, lk2_ref, subln_ref,
                      o_ref, acc_ref, *, lam_init):
    t = DIFF_TILE
    i = pl.program_id(2)
    qt = (q_ref[0].astype(F32) * (HALF_W ** -0.5 * LOG2E)).T
    qts = (qt[:HALF_W].astype(BF16), qt[HALF_W:].astype(BF16))
    acc_ref[...] = jnp.zeros_like(acc_ref)

    def all_scores(items):
        out = []
        for kb, _ in items:
            kc = k_ref[0, pl.ds(kb * t, t), :]
            out.append([jnp.dot(kc[:, half * HALF_W:(half + 1) * HALF_W], qts[half],
                                preferred_element_type=F32) for half in range(2)])
        return out

    def process(items, m):
        m = list(m)
        for (kb, slot), s_pair in zip(items, all_scores(items)):
            vt = vt_ref[0, 0, kb]
            for half in range(2):
                s = s_pair[half]
                if slot is not None:
                    s = s + bias_ref[0, slot]
                m_new = jnp.maximum(m[half], jnp.max(s, axis=0, keepdims=True))
                p = jnp.exp2(s - m_new)
                pv = jnp.dot(vt, p.astype(BF16), preferred_element_type=F32)
                acc_ref[half] = jnp.exp2(m[half] - m_new) * acc_ref[half] + pv
                m[half] = m_new
        return tuple(m)

    def process_stale(items, m):
        pvs, cms = [None, None], [None, None]
        for (kb, slot), s_pair in zip(items, all_scores(items)):
            vt = vt_ref[0, 0, kb]
            for half in range(2):
                s = s_pair[half]
                if slot is not None:
                    s = s + bias_ref[0, slot]
                cm = jnp.max(s, axis=0, keepdims=True)
                pv = jnp.dot(vt, jnp.exp2(s - m[half]).astype(BF16), preferred_element_type=F32)
                cms[half] = cm if cms[half] is None else jnp.maximum(cms[half], cm)
                pvs[half] = pv if pvs[half] is None else pvs[half] + pv
        excess = jnp.maximum(jnp.max(cms[0] - m[0]), jnp.max(cms[1] - m[1]))

        def commit(ops):
            m_old, cms_, pvs_ = ops
            out = []
            for half in range(2):
                m_new = jnp.maximum(m_old[half], cms_[half])
                acc_ref[half] = (acc_ref[half] + pvs_[half]) * jnp.exp2(m_old[half] - m_new)
                out.append(m_new)
            return tuple(out)

        def redo(ops):
            kb0, slot0 = items[0]
            return lax.fori_loop(
                0, len(items),
                lambda n, mm: process([(kb0 + n, None if slot0 is None else slot0 - n)], mm), ops[0])

        return lax.cond(jnp.logical_not(excess <= STALE_MAX_SLACK), redo, commit,
                        (tuple(m), tuple(cms), tuple(pvs)))

    neg = jnp.full((1, t), -jnp.inf, F32)
    n_far = jnp.maximum(i - 1, 0)
    m = lax.cond(n_far >= 1, lambda c: process([(0, None)], c), lambda c: c, (neg, neg))
    rest = jnp.maximum(n_far - 1, 0)
    one, two = rest % 2, (rest // 2) % 2
    m = lax.cond(one == 1, lambda c: process_stale([(1, None)], c), lambda c: c, m)
    m = lax.cond(two == 1, lambda c: process_stale([(1 + one, None), (2 + one, None)], c), lambda c: c, m)
    base = 1 + one + 2 * two
    m = lax.fori_loop(
        0, rest // 4, lambda j, c: process_stale([(base + 4 * j + r, None) for r in range(4)], c), m)
    lax.cond(i >= 1, lambda c: process_stale([(i - 1, 1), (i, 0)], c), lambda c: process([(0, 0)], c), m)

    lam = (jnp.exp(jnp.sum(lq1_ref[...] * lk1_ref[...])) - jnp.exp(jnp.sum(lq2_ref[...] * lk2_ref[...]))
           + lam_init)
    l1 = acc_ref[0, HEAD_W:HEAD_W + 1, :]
    l2 = acc_ref[1, HEAD_W:HEAD_W + 1, :]
    o = acc_ref[0, :HEAD_W, :] * (1.0 / l1) - acc_ref[1, :HEAD_W, :] * (lam / l2)
    o = o * lax.rsqrt(jnp.mean(o * o, axis=0, keepdims=True) + EPS) * subln_ref[...]
    o_ref[0] = (o * (1.0 - lam_init)).T.astype(o_ref.dtype)


def _value_proj_t_kernel(a_ref, w_ref, ssq_ref, o_ref, *, norm_dim):
    acc = jnp.dot(a_ref[...], w_ref[...].astype(BF16), preferred_element_type=F32)
    acc = acc * _row_rstd(ssq_ref[...], norm_dim)
    _, heads, chunks, _, t = o_ref.shape
    ones = jnp.ones((V_ROWS - HEAD_W, t), o_ref.dtype)
    for hh in range(heads):
        for c in range(chunks):
            blk = acc[c * t:(c + 1) * t, hh * HEAD_W:(hh + 1) * HEAD_W]
            o_ref[0, hh, c, :HEAD_W, :] = blk.T.astype(o_ref.dtype)
            o_ref[0, hh, c, HEAD_W:, :] = ones


def value_proj_t(h, ssq, w, col0, batch, seq, bm=2048, bn=512):
    m, k = h.shape
    t = DIFF_TILE
    nh = (w.shape[1] - col0) // HEAD_W
    bm = _pick(seq, bm)
    assert bm % t == 0 and bn % HEAD_W == 0 and col0 % bn == 0 and nh * HEAD_W % bn == 0
    tiles_per_seq = seq // bm
    return pl.pallas_call(
        functools.partial(_value_proj_t_kernel, norm_dim=k),
        grid=(m // bm, nh * HEAD_W // bn),
        in_specs=[pl.BlockSpec((bm, k), lambda i, j: (i, 0), pipeline_mode=pl.Buffered(1)),
                  pl.BlockSpec((k, bn), lambda i, j: (0, col0 // bn + j)),
                  pl.BlockSpec((bm, LANES), lambda i, j: (i, 0))],
        out_specs=pl.BlockSpec((1, bn // HEAD_W, bm // t, V_ROWS, t),
                               lambda i, j: (i // tiles_per_seq, j, i % tiles_per_seq, 0, 0)),
        out_shape=jax.ShapeDtypeStruct((batch, nh, seq // t, V_ROWS, t), BF16),
        compiler_params=_params(("parallel", "arbitrary")),
        name="value_proj_t",
    )(h, w, ssq)


def diff_attention(qproj, kproj, vt, bias, lq1, lk1, lq2, lk2, subln_g, lam_init):
    b, s, _ = qproj.shape
    nh = DIFF_HEADS
    t = DIFF_TILE
    assert s % t == 0
    nc = s // t
    vec = lambda a: a.astype(F32).reshape(1, -1)
    small = lambda n: pl.BlockSpec((1, n), lambda bi, h, qb: (0, 0))
    return pl.pallas_call(
        functools.partial(_diff_attn_kernel, lam_init=lam_init),
        grid=(b, nh, nc),
        in_specs=[pl.BlockSpec((1, t, HEAD_W), lambda bi, h, qb: (bi, qb, h)),
                  pl.BlockSpec((1, s, HEAD_W), lambda bi, h, qb: (bi, 0, h)),
                  pl.BlockSpec((1, 1, nc, V_ROWS, t), lambda bi, h, qb: (bi, h, 0, 0, 0)),
                  pl.BlockSpec((1, 2, t, t), lambda bi, h, qb: (h, 0, 0, 0)),
                  small(HALF_W), small(HALF_W), small(HALF_W), small(HALF_W),
                  pl.BlockSpec((HEAD_W, 1), lambda bi, h, qb: (0, 0))],
        out_specs=pl.BlockSpec((1, t, HEAD_W), lambda bi, h, qb: (bi, qb, h)),
        out_shape=jax.ShapeDtypeStruct((b, s, nh * HEAD_W), BF16),
        scratch_shapes=[pltpu.VMEM((2, V_ROWS, t), F32)],
        compiler_params=_params(("parallel", "parallel", "arbitrary")),
        name="diff_attention",
    )(qproj, kproj, vt, bias, vec(lq1), vec(lk1), vec(lq2), vec(lk2), subln_g.astype(F32).reshape(-1, 1))


def _up_conv_gate_kernel(h_ref, ssq_ref, wa_ref, wg_ref, cwa_ref, cwg_ref, cba_ref, cbg_ref, o_ref,
                         halo_a, halo_g, *, tiles_per_seq, chunks):
    first = (pl.program_id(1) % tiles_per_seq) == 0
    d = h_ref.shape[1]
    wa, wg = wa_ref[...], wg_ref[...]
    rstd = _row_rstd(ssq_ref[...], d)

    us, r0 = [], 0
    for rc in chunks:
        hc = h_ref[r0:r0 + rc, :]
        rs = rstd[r0:r0 + rc]
        us.append((jnp.dot(hc, wa, preferred_element_type=F32) * rs, jnp.dot(hc, wg, preferred_element_type=F32) * rs))
        r0 += rc

    row8 = lax.broadcasted_iota(jnp.int32, (8, wa.shape[1]), 0)

    def conv(u, prev, cw_ref, cb_ref):
        top = u[:8]
        t1 = jnp.where(row8 == 0, prev[7:8], pltpu.roll(top, 1, 0))
        t2 = jnp.where(row8 == 0, prev[6:7], jnp.where(row8 == 1, prev[7:8], pltpu.roll(top, 2, 0)))
        u1 = jnp.concatenate([t1, pltpu.roll(u, 1, 0)[8:]], axis=0)
        u2 = jnp.concatenate([t2, pltpu.roll(u, 2, 0)[8:]], axis=0)
        cw = cw_ref[...]
        return cb_ref[...] + cw[0:1] * u2 + cw[1:2] * u1 + cw[2:3] * u

    prev_a = jnp.where(first, 0.0, halo_a[...])
    prev_g = jnp.where(first, 0.0, halo_g[...])
    r0 = 0
    for rc, (ua, ug) in zip(chunks, us):
        a = conv(ua, prev_a, cwa_ref, cba_ref)
        g = conv(ug, prev_g, cwg_ref, cbg_ref)
        o_ref[r0:r0 + rc, :] = (a * jax.nn.sigmoid(a) * g).astype(o_ref.dtype)
        prev_a, prev_g = ua[rc - 8:], ug[rc - 8:]
        r0 += rc
    halo_a[...] = prev_a
    halo_g[...] = prev_g


def _row_chunks(bm, rc):
    rc = _pick(bm, rc)
    chunks = [rc] * (bm // rc)
    if len(chunks) > 1 and rc % 32 == 0:
        chunks[-1:] = [rc // 2, rc // 2]
    return tuple(chunks)


def up_conv_gate(h, ssq, w_up, conv_w, conv_b, layer, seq, bm=1024, bn=512, rc=256):
    m, d = h.shape
    f = w_up.shape[2] // 2
    bm, bn = _pick(seq, bm), _pick(f, bn)
    nj = f // bn

    def cols(rows_, off):
        return pl.BlockSpec((None, rows_, bn), lambda j, i: (layer, 0, off + j))

    return pl.pallas_call(
        functools.partial(_up_conv_gate_kernel, tiles_per_seq=seq // bm, chunks=_row_chunks(bm, rc)),
        grid=(nj, m // bm),
        in_specs=[pl.BlockSpec((bm, d), lambda j, i: (i, 0)),
                  pl.BlockSpec((bm, LANES), lambda j, i: (i, 0)),
                  cols(d, 0), cols(d, nj), cols(3, 0), cols(3, nj), cols(1, 0), cols(1, nj)],
        out_specs=pl.BlockSpec((bm, bn), lambda j, i: (i, j)),
        out_shape=jax.ShapeDtypeStruct((m, f), BF16),
        scratch_shapes=[pltpu.VMEM((8, bn), F32), pltpu.VMEM((8, bn), F32)],
        compiler_params=_params(("parallel", "arbitrary")),
        name="up_conv_gate",
    )(h, ssq, w_up, w_up, conv_w, conv_w, conv_b, conv_b)


def kernel(x, mem, a_w_in, b_w_q, w_kv, w_out, mem_w_kv, ffn_w_up, ffn_conv_w, ffn_conv_b, ffn_w_down,
           attn_norm, ffn_norm, kv_norm, mem_norm, final_norm, rel_bias,
           lambda_q1, lambda_k1, lambda_q2, lambda_k2, diff_subln):
    b, s, d = x.shape
    depth = w_out.shape[0]
    n_a = a_w_in.shape[0]
    m = b * s
    mix_w = RET_HEADS * HEAD_W
    mem_w = MEM_HEADS * HEAD_W
    assert d == mix_w + mem_w and a_w_in.shape[2] == 4 * mix_w + mem_w and b_w_q.shape[2] == mix_w + mem_w

    pos = jnp.arange(s, dtype=F32)
    inv_freq = ROPE_BASE ** (-jnp.arange(HALF_W, dtype=F32) / HALF_W)
    ang = pos[:, None] * inv_freq[None, :]
    cos, sin = jnp.cos(ang), jnp.sin(ang)

    xf = x.reshape(m, d)
    memn = rmsnorm(mem.reshape(-1, d), mem_norm, BF16)
    w_up = ffn_w_up.astype(BF16)
    w_down = ffn_w_down.astype(BF16)
    conv_b = ffn_conv_b[:, None, :]
    h, ssq = rmsnorm(xf, attn_norm[0], BF16), None
    hkv = kproj = vt = bias = None
    for layer in range(depth):
        if layer == n_a:
            kproj = matmul([hkv], w_kv[None], 0, BF16, ssq=ssq, n_cols=mix_w, name="key_proj").reshape(b, s, -1)
            vt = value_proj_t(hkv, ssq, w_kv, mix_w, b, s)
            bias = bias_tiles(rel_bias)
        mkv = matmul([memn], mem_w_kv, layer, BF16).reshape(b, -1, 2 * mem_w)
        if layer < n_a:
            proj = matmul([h], a_w_in, layer, BF16, ssq=ssq).reshape(b, s, -1)
            mix = retention(proj, cos, sin)
            mo = mem_attention(proj, mkv, (4 * mix_w) // mem_w)
        else:
            bl = layer - n_a
            lam_init = 0.8 - 0.6 * math.exp(-0.3 * layer)
            proj = matmul([h], b_w_q, bl, BF16, ssq=ssq).reshape(b, s, -1)
            mix = diff_attention(proj, kproj, vt, bias, lambda_q1[bl], lambda_k1[bl], lambda_q2[bl],
                                 lambda_k2[bl], diff_subln[bl], lam_init)
            mo = mem_attention(proj, mkv, mix_w // mem_w)
        xf, (h,), ssq = matmul([mix.reshape(m, mix_w), mo.reshape(m, mem_w)], w_out, layer, F32, resid=xf,
                               gains=[ffn_norm[layer]], bm=2048, bn=256, name="out_proj")
        act = up_conv_gate(h, ssq, w_up, ffn_conv_w, conv_b, layer, s)
        if layer + 1 == depth:
            xf = matmul([act], w_down, layer, F32, resid=xf, bm=1024, bn=512, name="down_proj")
        else:
            gains = [attn_norm[layer + 1]] + ([kv_norm] if layer + 1 == n_a else [])
            xf, hs, ssq = matmul([act], w_down, layer, F32, resid=xf, gains=gains,
                                 bm=1024, bn=512, name="down_proj")
            h, hkv = hs[0], hs[-1]
    return rmsnorm(xf, final_norm, F32).reshape(b, s, d)
```

```python
import functools
import math

import jax
import jax.numpy as jnp
from jax import lax
from jax.experimental import pallas as pl
from jax.experimental.pallas import tpu as pltpu

F32 = jnp.float32
BF16 = jnp.bfloat16

EPS = 1e-6
CHUNK = 64
ROPE_BASE = 10000.0
NUM_BUCKETS = 32
MAX_DISTANCE = 128
MEM_HEADS = 4
RET_HEADS = 12
DIFF_HEADS = 12
HEAD_W = 256
HALF_W = HEAD_W // 2

VMEM_LIMIT = 56 * 1024 * 1024


def _params(sem):
    return pltpu.CompilerParams(dimension_semantics=sem, vmem_limit_bytes=VMEM_LIMIT)


def _pick(n, pref):
    b = min(pref, n)
    while n % b:
        b //= 2
    return b


def _rmsnorm_kernel(x_ref, g_ref, o_ref):
    x = x_ref[...]
    ms = jnp.mean(x * x, axis=-1, keepdims=True)
    o_ref[...] = (x * lax.rsqrt(ms + EPS) * g_ref[...]).astype(o_ref.dtype)


def rmsnorm(x, g, out_dtype, bm=256):
    m, d = x.shape
    bm = _pick(m, bm)
    return pl.pallas_call(
        _rmsnorm_kernel,
        grid=(m // bm,),
        in_specs=[pl.BlockSpec((bm, d), lambda i: (i, 0)),
                  pl.BlockSpec((1, d), lambda i: (0, 0))],
        out_specs=pl.BlockSpec((bm, d), lambda i: (i, 0)),
        out_shape=jax.ShapeDtypeStruct((m, d), out_dtype),
        compiler_params=_params(("parallel",)),
        name="rmsnorm",
    )(x, g.reshape(1, d))


LANES = 128


def _row_rstd(ssq, norm_dim):
    return lax.rsqrt(jnp.sum(ssq, axis=1, keepdims=True) * (1.0 / norm_dim) + EPS)


def _lane_partial_ssq(x):
    sq = x * x
    part = sq[:, :LANES]
    for c in range(1, x.shape[1] // LANES):
        part = part + sq[:, c * LANES:(c + 1) * LANES]
    return part


def _mm_kernel(*refs, n_act, has_resid, has_rowscale, n_gain, norm_dim, rc):
    it = iter(refs)
    a_refs = [next(it) for _ in range(n_act)]
    w_refs = [next(it) for _ in range(n_act)]
    r_ref = next(it) if has_resid else None
    ssq_ref = next(it) if has_rowscale else None
    g_refs = [next(it) for _ in range(n_gain)]
    o_ref = next(it)
    xg_refs = [next(it) for _ in range(n_gain)]
    ssq_out = next(it) if n_gain else None

    bm, bn = o_ref.shape
    ws = [w_ref[...].astype(BF16) for w_ref in w_refs]
    accs = []
    for c in range(bm // rc):
        rows = slice(c * rc, (c + 1) * rc)
        acc = None
        for a_ref, w in zip(a_refs, ws):
            part = jnp.dot(a_ref[rows, :], w, preferred_element_type=F32)
            acc = part if acc is None else acc + part
        accs.append(acc)
    if n_gain:
        @pl.when(pl.program_id(1) == 0)
        def _():
            ssq_out[...] = jnp.zeros_like(ssq_out)

    for c, acc in enumerate(accs):
        rows = slice(c * rc, (c + 1) * rc)
        if has_rowscale:
            acc = acc * _row_rstd(ssq_ref[rows, :], norm_dim)
        if has_resid:
            acc = r_ref[rows, :] + acc
        o_ref[rows, :] = acc.astype(o_ref.dtype)
        for g_ref, xg_ref in zip(g_refs, xg_refs):
            xg_ref[rows, :] = (acc * g_ref[...]).astype(xg_ref.dtype)
        if n_gain:
            ssq_out[rows, :] += _lane_partial_ssq(acc)


def matmul(acts, w, layer, out_dtype, resid=None, ssq=None, gains=(), n_cols=None, bm=2048, bn=512, name="matmul"):
    m = acts[0].shape[0]
    k, n = w.shape[1], n_cols or w.shape[2]
    bm, bn = _pick(m, bm), _pick(n, bn)
    tile = pl.BlockSpec((bm, bn), lambda i, j: (i, j))
    rows = pl.BlockSpec((bm, LANES), lambda i, j: (i, 0))
    act_specs, w_specs, off = [], [], 0
    for a in acts:
        ki = a.shape[1]
        assert off % ki == 0
        act_specs.append(pl.BlockSpec((bm, ki), lambda i, j: (i, 0), pipeline_mode=pl.Buffered(1)))
        w_specs.append(pl.BlockSpec((None, ki, bn), lambda i, j, kb=off // ki: (layer, kb, j)))
        off += ki
    assert off == k
    in_specs = act_specs + w_specs
    args = list(acts) + [w] * len(acts)
    if resid is not None:
        in_specs.append(tile)
        args.append(resid)
    if ssq is not None:
        in_specs.append(rows)
        args.append(ssq)
    for g in gains:
        in_specs.append(pl.BlockSpec((1, bn), lambda i, j: (0, j)))
        args.append(g.reshape(1, n))
    out_specs = [tile] + [tile] * len(gains)
    out_shape = [jax.ShapeDtypeStruct((m, n), out_dtype)] + [jax.ShapeDtypeStruct((m, n), BF16)] * len(gains)
    if gains:
        out_specs.append(rows)
        out_shape.append(jax.ShapeDtypeStruct((m, LANES), F32))
    outs = pl.pallas_call(
        functools.partial(_mm_kernel, n_act=len(acts), has_resid=resid is not None,
                          has_rowscale=ssq is not None, n_gain=len(gains), norm_dim=k,
                          rc=bm),
        grid=(m // bm, n // bn),
        in_specs=in_specs,
        out_specs=out_specs,
        out_shape=out_shape,
        compiler_params=_params(("parallel", "arbitrary")),
        name=name,
    )(*args)
    if not gains:
        return outs[0]
    return outs[0], outs[1:-1], outs[-1]


def _retention_kernel(q_ref, k_ref, v_ref, g_ref, cos_ref, sin_ref, dmat_ref, qd_ref, kd_ref, cd_ref,
                      o_ref, state_ref, *, tblk):
    @pl.when(pl.program_id(2) == 0)
    def _():
        state_ref[...] = jnp.zeros_like(state_ref)

    kscale = HEAD_W ** -0.5
    qd = qd_ref[0]
    kd = kd_ref[0]
    state = state_ref[...]
    for sub in range(q_ref.shape[1] // tblk):
        rows = slice(sub * tblk, (sub + 1) * tblk)
        cos = cos_ref[rows, :]
        sin = sin_ref[rows, :]

        def rot(x):
            x1, x2 = x[:, :HALF_W], x[:, HALF_W:]
            return x1 * cos - x2 * sin, x2 * cos + x1 * sin

        q1, q2 = rot(q_ref[0, rows, :].astype(F32))
        k1, k2 = rot(k_ref[0, rows, :].astype(F32))
        k1, k2 = k1 * kscale, k2 * kscale
        v = v_ref[0, rows, :]

        qr = jnp.concatenate([q1, q2], axis=-1).astype(BF16)
        kr = jnp.concatenate([k1, k2], axis=-1).astype(BF16)
        qdec = jnp.concatenate([q1 * qd, q2 * qd], axis=-1).astype(BF16)
        kdec = jnp.concatenate([k1 * kd, k2 * kd], axis=-1)

        scores = lax.dot_general(qr, kr, (((1,), (1,)), ((), ())), preferred_element_type=F32)
        scores = (scores * dmat_ref[0]).astype(BF16)
        o = jnp.dot(scores, v, preferred_element_type=F32)
        o = o + jnp.dot(qdec, state.astype(BF16), preferred_element_type=F32)
        kv = jnp.dot(kdec.T.astype(BF16), v, preferred_element_type=F32)
        state = state * cd_ref[0] + kv

        mu = jnp.mean(o, axis=-1, keepdims=True)
        oc = o - mu
        on = oc * lax.rsqrt(jnp.mean(oc * oc, axis=-1, keepdims=True) + EPS)
        g = g_ref[0, rows, :].astype(F32)
        o_ref[0, rows, :] = (g * jax.nn.sigmoid(g) * on).astype(o_ref.dtype)
    state_ref[...] = state


def _retention_tables(tblk):
    h = jnp.arange(RET_HEADS, dtype=F32)
    log_gamma = jnp.log(1.0 - 2.0 ** (-5.0 - h))
    idx = jnp.arange(tblk, dtype=F32)
    ci = jnp.arange(tblk) // CHUNK
    diff = idx[:, None] - idx[None, :]
    same = ci[:, None] == ci[None, :]
    earlier = ci[None, :] < ci[:, None]
    expo = jnp.where(same, jnp.abs(diff), diff)
    dmat = jnp.where((same | earlier)[None], jnp.exp(log_gamma[:, None, None] * expo[None]), 0.0)
    qd = jnp.exp((idx[None, :] + 1.0) * log_gamma[:, None])
    kd = jnp.exp((tblk - 1.0 - idx[None, :]) * log_gamma[:, None])
    cd = jnp.exp(tblk * log_gamma)
    bc = lambda a: jnp.broadcast_to(a[:, :, None], (RET_HEADS, tblk, HALF_W))
    cdv = jnp.broadcast_to(cd[:, None, None], (RET_HEADS, 1, HEAD_W))
    return dmat, bc(qd), bc(kd), cdv


def retention(proj, cos, sin, tblk=256, nsub=4):
    b, s, _ = proj.shape
    tblk = _pick(s, tblk)
    assert tblk % CHUNK == 0
    step = _pick(s, tblk * nsub)
    dmat, qd, kd, cdv = _retention_tables(tblk)
    nh = RET_HEADS

    def col(off):
        return pl.BlockSpec((1, step, HEAD_W), lambda bi, h, t: (bi, t, off + h))

    def per_head(shape):
        return pl.BlockSpec((1,) + shape, lambda bi, h, t: (h, 0, 0))

    return pl.pallas_call(
        functools.partial(_retention_kernel, tblk=tblk),
        grid=(b, nh, s // step),
        in_specs=[col(0), col(nh), col(2 * nh), col(3 * nh),
                  pl.BlockSpec((step, HALF_W), lambda bi, h, t: (t, 0)),
                  pl.BlockSpec((step, HALF_W), lambda bi, h, t: (t, 0)),
                  per_head((tblk, tblk)), per_head((tblk, HALF_W)), per_head((tblk, HALF_W)),
                  per_head((1, HEAD_W))],
        out_specs=pl.BlockSpec((1, step, HEAD_W), lambda bi, h, t: (bi, t, h)),
        out_shape=jax.ShapeDtypeStruct((b, s, nh * HEAD_W), BF16),
        scratch_shapes=[pltpu.VMEM((HEAD_W, HEAD_W), F32)],
        compiler_params=_params(("parallel", "parallel", "arbitrary")),
        name="retention",
    )(proj, proj, proj, proj, cos, sin, dmat, qd, kd, cdv)


def _mem_attn_kernel(q_ref, kv_ref, o_ref):
    scale = HEAD_W ** -0.5
    width = MEM_HEADS * HEAD_W
    for h in range(MEM_HEADS):
        q = q_ref[0, :, h * HEAD_W:(h + 1) * HEAD_W]
        mk = kv_ref[0, :, h * HEAD_W:(h + 1) * HEAD_W]
        mv = kv_ref[0, :, width + h * HEAD_W:width + (h + 1) * HEAD_W]
        logits = lax.dot_general(q, mk, (((1,), (1,)), ((), ())), preferred_element_type=F32) * scale
        mx = jnp.max(logits, axis=-1, keepdims=True)
        p = jnp.exp(logits - mx)
        p = p / jnp.sum(p, axis=-1, keepdims=True)
        o = jnp.dot(p.astype(BF16), mv, preferred_element_type=F32)
        o_ref[0, :, h * HEAD_W:(h + 1) * HEAD_W] = o.astype(o_ref.dtype)


def mem_attention(proj, mkv, col_block, tq=1024):
    b, s, _ = proj.shape
    mt = mkv.shape[1]
    width = MEM_HEADS * HEAD_W
    tq = _pick(s, tq)
    return pl.pallas_call(
        _mem_attn_kernel,
        grid=(b, s // tq),
        in_specs=[pl.BlockSpec((1, tq, width), lambda bi, t: (bi, t, col_block)),
                  pl.BlockSpec((1, mt, 2 * width), lambda bi, t: (bi, 0, 0))],
        out_specs=pl.BlockSpec((1, tq, width), lambda bi, t: (bi, t, 0)),
        out_shape=jax.ShapeDtypeStruct((b, s, width), BF16),
        compiler_params=_params(("parallel", "parallel")),
        name="mem_attention",
    )(proj, mkv)


DIFF_TILE = 512
LOG2E = math.log2(math.e)


def _t5_bucket(rel):
    half = NUM_BUCKETS // 2
    max_exact = half // 2
    ret = jnp.where(rel > 0, half, 0)
    n = jnp.abs(rel)
    nf = jnp.maximum(n, 1).astype(F32)
    large = max_exact + (jnp.log(nf / max_exact) / math.log(MAX_DISTANCE / max_exact)
                         * (half - max_exact)).astype(jnp.int32)
    large = jnp.minimum(large, half - 1)
    return ret + jnp.where(n < max_exact, n, large)


def _bias_buckets(t):
    i = jnp.arange(t)
    rel = i[:, None] - i[None, :]
    bucket = _t5_bucket(jnp.stack([rel, rel - t]))
    masked = (i[:, None] // CHUNK) > (i[None, :] // CHUNK)
    return bucket.at[0].set(jnp.where(masked, NUM_BUCKETS, bucket[0]))


def _bias_kernel(table_ref, bucket_ref, o_ref):
    h = pl.program_id(0)
    far = table_ref[NUM_BUCKETS // 2 - 1, h]
    rows = 64

    def body(c, carry):
        for slot in range(2):
            b = bucket_ref[slot, pl.ds(c * rows, rows), :]
            acc = jnp.full(b.shape, -jnp.inf, F32)
            for i in range(NUM_BUCKETS):
                acc = jnp.where(b == i, (table_ref[i, h] - far) * LOG2E, acc)
            o_ref[0, slot, pl.ds(c * rows, rows), :] = acc
        return carry

    lax.fori_loop(0, bucket_ref.shape[1] // rows, body, 0)


def bias_tiles(rel_bias):
    t = DIFF_TILE
    assert t >= MAX_DISTANCE and t % CHUNK == 0
    nh = rel_bias.shape[1]
    return pl.pallas_call(
        _bias_kernel,
        grid=(nh,),
        in_specs=[pl.BlockSpec(memory_space=pltpu.SMEM),
                  pl.BlockSpec((2, t, t), lambda h: (0, 0, 0))],
        out_specs=pl.BlockSpec((1, 2, t, t), lambda h: (h, 0, 0, 0)),
        out_shape=jax.ShapeDtypeStruct((nh, 2, t, t), F32),
        compiler_params=_params(("parallel",)),
        name="bias_tiles",
    )(rel_bias.astype(F32), _bias_buckets(t))


V_ROWS = HEAD_W + 16
STALE_MAX_SLACK = 64.0


def _diff_attn_kernel(q_ref, k_ref, vt_ref, bias_ref, lq1_ref, lk1_ref, lq2_ref, lk2_ref, subln_ref,
                      o_ref, acc_ref, *, lam_init):
    t = DIFF_TILE
    i = pl.program_id(2)
    qt = (q_ref[0].astype(F32) * (HALF_W ** -0.5 * LOG2E)).T
    qts = (qt[:HALF_W].astype(BF16), qt[HALF_W:].astype(BF16))
    acc_ref[...] = jnp.zeros_like(acc_ref)

    def all_scores(items):
        out = []
        for kb, _ in items:
            kc = k_ref[0, pl.ds(kb * t, t), :]
            out.append([jnp.dot(kc[:, half * HALF_W:(half + 1) * HALF_W], qts[half],
                                preferred_element_type=F32) for half in range(2)])
        return out

    def process(items, m):
        m = list(m)
        for (kb, slot), s_pair in zip(items, all_scores(items)):
            vt = vt_ref[0, 0, kb]
            for half in range(2):
                s = s_pair[half]
                if slot is not None:
                    s = s + bias_ref[0, slot]
                m_new = jnp.maximum(m[half], jnp.max(s, axis=0, keepdims=True))
                p = jnp.exp2(s - m_new)
                pv = jnp.dot(vt, p.astype(BF16), preferred_element_type=F32)
                acc_ref[half] = jnp.exp2(m[half] - m_new) * acc_ref[half] + pv
                m[half] = m_new
        return tuple(m)

    def process_stale(items, m):
        pvs, cms = [None, None], [None, None]
        for (kb, slot), s_pair in zip(items, all_scores(items)):
            vt = vt_ref[0, 0, kb]
            for half in range(2):
                s = s_pair[half]
                if slot is not None:
                    s = s + bias_ref[0, slot]
                cm = jnp.max(s, axis=0, keepdims=True)
                pv = jnp.dot(vt, jnp.exp2(s - m[half]).astype(BF16), preferred_element_type=F32)
                cms[half] = cm if cms[half] is None else jnp.maximum(cms[half], cm)
                pvs[half] = pv if pvs[half] is None else pvs[half] + pv
        excess = jnp.maximum(jnp.max(cms[0] - m[0]), jnp.max(cms[1] - m[1]))

        def commit(ops):
            m_old, cms_, pvs_ = ops
            out = []
            for half in range(2):
                m_new = jnp.maximum(m_old[half], cms_[half])
                acc_ref[half] = (acc_ref[half] + pvs_[half]) * jnp.exp2(m_old[half] - m_new)
                out.append(m_new)
            return tuple(out)

        def redo(ops):
            kb0, slot0 = items[0]
            return lax.fori_loop(
                0, len(items),
                lambda n, mm: process([(kb0 + n, None if slot0 is None else slot0 - n)], mm), ops[0])

        return lax.cond(jnp.logical_not(excess <= STALE_MAX_SLACK), redo, commit,
                        (tuple(m), tuple(cms), tuple(pvs)))

    neg = jnp.full((1, t), -jnp.inf, F32)
    n_far = jnp.maximum(i - 1, 0)
    m = lax.cond(n_far >= 1, lambda c: process([(0, None)], c), lambda c: c, (neg, neg))
    rest = jnp.maximum(n_far - 1, 0)
    one, two = rest % 2, (rest // 2) % 2
    m = lax.cond(one == 1, lambda c: process_stale([(1, None)], c), lambda c: c, m)
    m = lax.cond(two == 1, lambda c: process_stale([(1 + one, None), (2 + one, None)], c), lambda c: c, m)
    base = 1 + one + 2 * two
    m = lax.fori_loop(
        0, rest // 4, lambda j, c: process_stale([(base + 4 * j + r, None) for r in range(4)], c), m)
    lax.cond(i >= 1, lambda c: process_stale([(i - 1, 1), (i, 0)], c), lambda c: process([(0, 0)], c), m)

    lam = (jnp.exp(jnp.sum(lq1_ref[...] * lk1_ref[...])) - jnp.exp(jnp.sum(lq2_ref[...] * lk2_ref[...]))
           + lam_init)
    l1 = acc_ref[0, HEAD_W:HEAD_W + 1, :]
    l2 = acc_ref[1, HEAD_W:HEAD_W + 1, :]
    o = acc_ref[0, :HEAD_W, :] * (1.0 / l1) - acc_ref[1, :HEAD_W, :] * (lam / l2)
    o = o * lax.rsqrt(jnp.mean(o * o, axis=0, keepdims=True) + EPS) * subln_ref[...]
    o_ref[0] = (o * (1.0 - lam_init)).T.astype(o_ref.dtype)


def _value_proj_t_kernel(a_ref, w_ref, ssq_ref, o_ref, *, norm_dim):
    acc = jnp.dot(a_ref[...], w_ref[...].astype(BF16), preferred_element_type=F32)
    acc = acc * _row_rstd(ssq_ref[...], norm_dim)
    _, heads, chunks, _, t = o_ref.shape
    ones = jnp.ones((V_ROWS - HEAD_W, t), o_ref.dtype)
    for hh in range(heads):
        for c in range(chunks):
            blk = acc[c * t:(c + 1) * t, hh * HEAD_W:(hh + 1) * HEAD_W]
            o_ref[0, hh, c, :HEAD_W, :] = blk.T.astype(o_ref.dtype)
            o_ref[0, hh, c, HEAD_W:, :] = ones


def value_proj_t(h, ssq, w, col0, batch, seq, bm=2048, bn=512):
    m, k = h.shape
    t = DIFF_TILE
    nh = (w.shape[1] - col0) // HEAD_W
    bm = _pick(seq, bm)
    assert bm % t == 0 and bn % HEAD_W == 0 and col0 % bn == 0 and nh * HEAD_W % bn == 0
    tiles_per_seq = seq // bm
    return pl.pallas_call(
        functools.partial(_value_proj_t_kernel, norm_dim=k),
        grid=(m // bm, nh * HEAD_W // bn),
        in_specs=[pl.BlockSpec((bm, k), lambda i, j: (i, 0), pipeline_mode=pl.Buffered(1)),
                  pl.BlockSpec((k, bn), lambda i, j: (0, col0 // bn + j)),
                  pl.BlockSpec((bm, LANES), lambda i, j: (i, 0))],
        out_specs=pl.BlockSpec((1, bn // HEAD_W, bm // t, V_ROWS, t),
                               lambda i, j: (i // tiles_per_seq, j, i % tiles_per_seq, 0, 0)),
        out_shape=jax.ShapeDtypeStruct((batch, nh, seq // t, V_ROWS, t), BF16),
        compiler_params=_params(("parallel", "arbitrary")),
        name="value_proj_t",
    )(h, w, ssq)


def diff_attention(qproj, kproj, vt, bias, lq1, lk1, lq2, lk2, subln_g, lam_init):
    b, s, _ = qproj.shape
    nh = DIFF_HEADS
    t = DIFF_TILE
    assert s % t == 0
    nc = s // t
    vec = lambda a: a.astype(F32).reshape(1, -1)
    small = lambda n: pl.BlockSpec((1, n), lambda bi, h, qb: (0, 0))
    return pl.pallas_call(
        functools.partial(_diff_attn_kernel, lam_init=lam_init),
        grid=(b, nh, nc),
        in_specs=[pl.BlockSpec((1, t, HEAD_W), lambda bi, h, qb: (bi, qb, h)),
                  pl.BlockSpec((1, s, HEAD_W), lambda bi, h, qb: (bi, 0, h)),
                  pl.BlockSpec((1, 1, nc, V_ROWS, t), lambda bi, h, qb: (bi, h, 0, 0, 0)),
                  pl.BlockSpec((1, 2, t, t), lambda bi, h, qb: (h, 0, 0, 0)),
                  small(HALF_W), small(HALF_W), small(HALF_W), small(HALF_W),
                  pl.BlockSpec((HEAD_W, 1), lambda bi, h, qb: (0, 0))],
        out_specs=pl.BlockSpec((1, t, HEAD_W), lambda bi, h, qb: (bi, qb, h)),
        out_shape=jax.ShapeDtypeStruct((b, s, nh * HEAD_W), BF16),
        scratch_shapes=[pltpu.VMEM((2, V_ROWS, t), F32)],
        compiler_params=_params(("parallel", "parallel", "arbitrary")),
        name="diff_attention",
    )(qproj, kproj, vt, bias, vec(lq1), vec(lk1), vec(lq2), vec(lk2), subln_g.astype(F32).reshape(-1, 1))


def _up_conv_gate_kernel(h_ref, ssq_ref, wa_ref, wg_ref, cwa_ref, cwg_ref, cba_ref, cbg_ref, o_ref,
                         halo_a, halo_g, *, tiles_per_seq, chunks):
    first = (pl.program_id(1) % tiles_per_seq) == 0
    d = h_ref.shape[1]
    wa, wg = wa_ref[...], wg_ref[...]
    rstd = _row_rstd(ssq_ref[...], d)

    us, r0 = [], 0
    for rc in chunks:
        hc = h_ref[r0:r0 + rc, :]
        rs = rstd[r0:r0 + rc]
        us.append((jnp.dot(hc, wa, preferred_element_type=F32) * rs, jnp.dot(hc, wg, preferred_element_type=F32) * rs))
        r0 += rc

    row8 = lax.broadcasted_iota(jnp.int32, (8, wa.shape[1]), 0)

    def conv(u, prev, cw_ref, cb_ref):
        top = u[:8]
        t1 = jnp.where(row8 == 0, prev[7:8], pltpu.roll(top, 1, 0))
        t2 = jnp.where(row8 == 0, prev[6:7], jnp.where(row8 == 1, prev[7:8], pltpu.roll(top, 2, 0)))
        u1 = jnp.concatenate([t1, pltpu.roll(u, 1, 0)[8:]], axis=0)
        u2 = jnp.concatenate([t2, pltpu.roll(u, 2, 0)[8:]], axis=0)
        cw = cw_ref[...]
        return cb_ref[...] + cw[0:1] * u2 + cw[1:2] * u1 + cw[2:3] * u

    prev_a = jnp.where(first, 0.0, halo_a[...])
    prev_g = jnp.where(first, 0.0, halo_g[...])
    r0 = 0
    for rc, (ua, ug) in zip(chunks, us):
        a = conv(ua, prev_a, cwa_ref, cba_ref)
        g = conv(ug, prev_g, cwg_ref, cbg_ref)
        o_ref[r0:r0 + rc, :] = (a * jax.nn.sigmoid(a) * g).astype(o_ref.dtype)
        prev_a, prev_g = ua[rc - 8:], ug[rc - 8:]
        r0 += rc
    halo_a[...] = prev_a
    halo_g[...] = prev_g


def _row_chunks(bm, rc):
    rc = _pick(bm, rc)
    return (rc,) * (bm // rc)


def up_conv_gate(h, ssq, w_up, conv_w, conv_b, layer, seq, bm=1024, bn=512, rc=256):
    m, d = h.shape
    f = w_up.shape[2] // 2
    bm, bn = _pick(seq, bm), _pick(f, bn)
    nj = f // bn

    def cols(rows_, off):
        return pl.BlockSpec((None, rows_, bn), lambda j, i: (layer, 0, off + j))

    return pl.pallas_call(
        functools.partial(_up_conv_gate_kernel, tiles_per_seq=seq // bm, chunks=_row_chunks(bm, rc)),
        grid=(nj, m // bm),
        in_specs=[pl.BlockSpec((bm, d), lambda j, i: (i, 0)),
                  pl.BlockSpec((bm, LANES), lambda j, i: (i, 0)),
                  cols(d, 0), cols(d, nj), cols(3, 0), cols(3, nj), cols(1, 0), cols(1, nj)],
        out_specs=pl.BlockSpec((bm, bn), lambda j, i: (i, j)),
        out_shape=jax.ShapeDtypeStruct((m, f), BF16),
        scratch_shapes=[pltpu.VMEM((8, bn), F32), pltpu.VMEM((8, bn), F32)],
        compiler_params=_params(("parallel", "arbitrary")),
        name="up_conv_gate",
    )(h, ssq, w_up, w_up, conv_w, conv_w, conv_b, conv_b)


def kernel(x, mem, a_w_in, b_w_q, w_kv, w_out, mem_w_kv, ffn_w_up, ffn_conv_w, ffn_conv_b, ffn_w_down,
           attn_norm, ffn_norm, kv_norm, mem_norm, final_norm, rel_bias,
           lambda_q1, lambda_k1, lambda_q2, lambda_k2, diff_subln):
    b, s, d = x.shape
    depth = w_out.shape[0]
    n_a = a_w_in.shape[0]
    m = b * s
    mix_w = RET_HEADS * HEAD_W
    mem_w = MEM_HEADS * HEAD_W
    assert d == mix_w + mem_w and a_w_in.shape[2] == 4 * mix_w + mem_w and b_w_q.shape[2] == mix_w + mem_w

    pos = jnp.arange(s, dtype=F32)
    inv_freq = ROPE_BASE ** (-jnp.arange(HALF_W, dtype=F32) / HALF_W)
    ang = pos[:, None] * inv_freq[None, :]
    cos, sin = jnp.cos(ang), jnp.sin(ang)

    xf = x.reshape(m, d)
    memn = rmsnorm(mem.reshape(-1, d), mem_norm, BF16)
    w_up = ffn_w_up.astype(BF16)
    w_down = ffn_w_down.astype(BF16)
    conv_b = ffn_conv_b[:, None, :]
    h, ssq = rmsnorm(xf, attn_norm[0], BF16), None
    hkv = kproj = vt = bias = None
    for layer in range(depth):
        if layer == n_a:
            kproj = matmul([hkv], w_kv[None], 0, BF16, ssq=ssq, n_cols=mix_w, name="key_proj").reshape(b, s, -1)
            vt = value_proj_t(hkv, ssq, w_kv, mix_w, b, s)
            bias = bias_tiles(rel_bias)
        mkv = matmul([memn], mem_w_kv, layer, BF16).reshape(b, -1, 2 * mem_w)
        if layer < n_a:
            proj = matmul([h], a_w_in, layer, BF16, ssq=ssq).reshape(b, s, -1)
            mix = retention(proj, cos, sin)
            mo = mem_attention(proj, mkv, (4 * mix_w) // mem_w)
        else:
            bl = layer - n_a
            lam_init = 0.8 - 0.6 * math.exp(-0.3 * layer)
            proj = matmul([h], b_w_q, bl, BF16, ssq=ssq).reshape(b, s, -1)
            mix = diff_attention(proj, kproj, vt, bias, lambda_q1[bl], lambda_k1[bl], lambda_q2[bl],
                                 lambda_k2[bl], diff_subln[bl], lam_init)
            mo = mem_attention(proj, mkv, mix_w // mem_w)
        xf, (h,), ssq = matmul([mix.reshape(m, mix_w), mo.reshape(m, mem_w)], w_out, layer, F32, resid=xf,
                               gains=[ffn_norm[layer]], bm=2048, bn=256, name="out_proj")
        act = up_conv_gate(h, ssq, w_up, ffn_conv_w, conv_b, layer, s)
        if layer + 1 == depth:
            xf = matmul([act], w_down, layer, F32, resid=xf, bm=1024, bn=512, name="down_proj")
        else:
            gains = [attn_norm[layer + 1]] + ([kv_norm] if layer + 1 == n_a else [])
            xf, hs, ssq = matmul([act], w_down, layer, F32, resid=xf, gains=gains,
                                 bm=1024, bn=512, name="down_proj")
            h, hkv = hs[0], hs[-1]
    return rmsnorm(xf, final_norm, F32).reshape(b, s, d)
```

```python
import functools
import math

import jax
import jax.numpy as jnp
from jax import lax
from jax.experimental import pallas as pl
from jax.experimental.pallas import tpu as pltpu

F32 = jnp.float32
BF16 = jnp.bfloat16

EPS = 1e-6
CHUNK = 64
ROPE_BASE = 10000.0
NUM_BUCKETS = 32
MAX_DISTANCE = 128
MEM_HEADS = 4
RET_HEADS = 12
DIFF_HEADS = 12
HEAD_W = 256
HALF_W = HEAD_W // 2

VMEM_LIMIT = 56 * 1024 * 1024


def _params(sem):
    return pltpu.CompilerParams(dimension_semantics=sem, vmem_limit_bytes=VMEM_LIMIT)


def _pick(n, pref):
    b = min(pref, n)
    while n % b:
        b //= 2
    return b


def _rmsnorm_kernel(x_ref, g_ref, o_ref):
    x = x_ref[...]
    ms = jnp.mean(x * x, axis=-1, keepdims=True)
    o_ref[...] = (x * lax.rsqrt(ms + EPS) * g_ref[...]).astype(o_ref.dtype)


def rmsnorm(x, g, out_dtype, bm=256):
    m, d = x.shape
    bm = _pick(m, bm)
    return pl.pallas_call(
        _rmsnorm_kernel,
        grid=(m // bm,),
        in_specs=[pl.BlockSpec((bm, d), lambda i: (i, 0)),
                  pl.BlockSpec((1, d), lambda i: (0, 0))],
        out_specs=pl.BlockSpec((bm, d), lambda i: (i, 0)),
        out_shape=jax.ShapeDtypeStruct((m, d), out_dtype),
        compiler_params=_params(("parallel",)),
        name="rmsnorm",
    )(x, g.reshape(1, d))


LANES = 128


def _row_rstd(ssq, norm_dim):
    return lax.rsqrt(jnp.sum(ssq, axis=1, keepdims=True) * (1.0 / norm_dim) + EPS)


def _lane_partial_ssq(x):
    sq = x * x
    part = sq[:, :LANES]
    for c in range(1, x.shape[1] // LANES):
        part = part + sq[:, c * LANES:(c + 1) * LANES]
    return part


def _mm_kernel(*refs, n_act, has_resid, has_rowscale, n_gain, norm_dim, rc):
    it = iter(refs)
    a_refs = [next(it) for _ in range(n_act)]
    w_refs = [next(it) for _ in range(n_act)]
    r_ref = next(it) if has_resid else None
    ssq_ref = next(it) if has_rowscale else None
    g_refs = [next(it) for _ in range(n_gain)]
    o_ref = next(it)
    xg_refs = [next(it) for _ in range(n_gain)]
    ssq_out = next(it) if n_gain else None

    bm, bn = o_ref.shape
    ws = [w_ref[...].astype(BF16) for w_ref in w_refs]
    accs = []
    for c in range(bm // rc):
        rows = slice(c * rc, (c + 1) * rc)
        acc = None
        for a_ref, w in zip(a_refs, ws):
            part = jnp.dot(a_ref[rows, :], w, preferred_element_type=F32)
            acc = part if acc is None else acc + part
        accs.append(acc)
    if n_gain:
        @pl.when(pl.program_id(1) == 0)
        def _():
            ssq_out[...] = jnp.zeros_like(ssq_out)

    for c, acc in enumerate(accs):
        rows = slice(c * rc, (c + 1) * rc)
        if has_rowscale:
            acc = acc * _row_rstd(ssq_ref[rows, :], norm_dim)
        if has_resid:
            acc = r_ref[rows, :] + acc
        o_ref[rows, :] = acc.astype(o_ref.dtype)
        for g_ref, xg_ref in zip(g_refs, xg_refs):
            xg_ref[rows, :] = (acc * g_ref[...]).astype(xg_ref.dtype)
        if n_gain:
            ssq_out[rows, :] += _lane_partial_ssq(acc)


def matmul(acts, w, layer, out_dtype, resid=None, ssq=None, gains=(), n_cols=None, bm=2048, bn=512, name="matmul"):
    m = acts[0].shape[0]
    k, n = w.shape[1], n_cols or w.shape[2]
    bm, bn = _pick(m, bm), _pick(n, bn)
    tile = pl.BlockSpec((bm, bn), lambda i, j: (i, j))
    rows = pl.BlockSpec((bm, LANES), lambda i, j: (i, 0))
    act_specs, w_specs, off = [], [], 0
    for a in acts:
        ki = a.shape[1]
        assert off % ki == 0
        act_specs.append(pl.BlockSpec((bm, ki), lambda i, j: (i, 0), pipeline_mode=pl.Buffered(1)))
        w_specs.append(pl.BlockSpec((None, ki, bn), lambda i, j, kb=off // ki: (layer, kb, j)))
        off += ki
    assert off == k
    in_specs = act_specs + w_specs
    args = list(acts) + [w] * len(acts)
    if resid is not None:
        in_specs.append(tile)
        args.append(resid)
    if ssq is not None:
        in_specs.append(rows)
        args.append(ssq)
    for g in gains:
        in_specs.append(pl.BlockSpec((1, bn), lambda i, j: (0, j)))
        args.append(g.reshape(1, n))
    out_specs = [tile] + [tile] * len(gains)
    out_shape = [jax.ShapeDtypeStruct((m, n), out_dtype)] + [jax.ShapeDtypeStruct((m, n), BF16)] * len(gains)
    if gains:
        out_specs.append(rows)
        out_shape.append(jax.ShapeDtypeStruct((m, LANES), F32))
    outs = pl.pallas_call(
        functools.partial(_mm_kernel, n_act=len(acts), has_resid=resid is not None,
                          has_rowscale=ssq is not None, n_gain=len(gains), norm_dim=k,
                          rc=bm),
        grid=(m // bm, n // bn),
        in_specs=in_specs,
        out_specs=out_specs,
        out_shape=out_shape,
        compiler_params=_params(("parallel", "arbitrary")),
        name=name,
    )(*args)
    if not gains:
        return outs[0]
    return outs[0], outs[1:-1], outs[-1]


def _retention_kernel(q_ref, k_ref, v_ref, g_ref, cos_ref, sin_ref, dmat_ref, qd_ref, kd_ref, cd_ref,
                      o_ref, state_ref, *, tblk):
    @pl.when(pl.program_id(2) == 0)
    def _():
        state_ref[...] = jnp.zeros_like(state_ref)

    kscale = HEAD_W ** -0.5
    qd = qd_ref[0]
    kd = kd_ref[0]
    state = state_ref[...]
    for sub in range(q_ref.shape[1] // tblk):
        rows = slice(sub * tblk, (sub + 1) * tblk)
        cos = cos_ref[rows, :]
        sin = sin_ref[rows, :]

        def rot(x):
            x1, x2 = x[:, :HALF_W], x[:, HALF_W:]
            return x1 * cos - x2 * sin, x2 * cos + x1 * sin

        q1, q2 = rot(q_ref[0, rows, :].astype(F32))
        k1, k2 = rot(k_ref[0, rows, :].astype(F32))
        k1, k2 = k1 * kscale, k2 * kscale
        v = v_ref[0, rows, :]

        qr = jnp.concatenate([q1, q2], axis=-1).astype(BF16)
        kr = jnp.concatenate([k1, k2], axis=-1).astype(BF16)
        qdec = jnp.concatenate([q1 * qd, q2 * qd], axis=-1).astype(BF16)
        kdec = jnp.concatenate([k1 * kd, k2 * kd], axis=-1)

        scores = lax.dot_general(qr, kr, (((1,), (1,)), ((), ())), preferred_element_type=F32)
        scores = (scores * dmat_ref[0]).astype(BF16)
        o = jnp.dot(scores, v, preferred_element_type=F32)
        o = o + jnp.dot(qdec, state.astype(BF16), preferred_element_type=F32)
        kv = jnp.dot(kdec.T.astype(BF16), v, preferred_element_type=F32)
        state = state * cd_ref[0] + kv

        mu = jnp.mean(o, axis=-1, keepdims=True)
        oc = o - mu
        on = oc * lax.rsqrt(jnp.mean(oc * oc, axis=-1, keepdims=True) + EPS)
        g = g_ref[0, rows, :].astype(F32)
        o_ref[0, rows, :] = (g * jax.nn.sigmoid(g) * on).astype(o_ref.dtype)
    state_ref[...] = state


def _retention_tables(tblk):
    h = jnp.arange(RET_HEADS, dtype=F32)
    log_gamma = jnp.log(1.0 - 2.0 ** (-5.0 - h))
    idx = jnp.arange(tblk, dtype=F32)
    ci = jnp.arange(tblk) // CHUNK
    diff = idx[:, None] - idx[None, :]
    same = ci[:, None] == ci[None, :]
    earlier = ci[None, :] < ci[:, None]
    expo = jnp.where(same, jnp.abs(diff), diff)
    dmat = jnp.where((same | earlier)[None], jnp.exp(log_gamma[:, None, None] * expo[None]), 0.0)
    qd = jnp.exp((idx[None, :] + 1.0) * log_gamma[:, None])
    kd = jnp.exp((tblk - 1.0 - idx[None, :]) * log_gamma[:, None])
    cd = jnp.exp(tblk * log_gamma)
    bc = lambda a: jnp.broadcast_to(a[:, :, None], (RET_HEADS, tblk, HALF_W))
    cdv = jnp.broadcast_to(cd[:, None, None], (RET_HEADS, 1, HEAD_W))
    return dmat, bc(qd), bc(kd), cdv


def retention(proj, cos, sin, tblk=256, nsub=8):
    b, s, _ = proj.shape
    tblk = _pick(s, tblk)
    assert tblk % CHUNK == 0
    step = _pick(s, tblk * nsub)
    dmat, qd, kd, cdv = _retention_tables(tblk)
    nh = RET_HEADS

    def col(off):
        return pl.BlockSpec((1, step, HEAD_W), lambda bi, h, t: (bi, t, off + h))

    def per_head(shape):
        return pl.BlockSpec((1,) + shape, lambda bi, h, t: (h, 0, 0))

    return pl.pallas_call(
        functools.partial(_retention_kernel, tblk=tblk),
        grid=(b, nh, s // step),
        in_specs=[col(0), col(nh), col(2 * nh), col(3 * nh),
                  pl.BlockSpec((step, HALF_W), lambda bi, h, t: (t, 0)),
                  pl.BlockSpec((step, HALF_W), lambda bi, h, t: (t, 0)),
                  per_head((tblk, tblk)), per_head((tblk, HALF_W)), per_head((tblk, HALF_W)),
                  per_head((1, HEAD_W))],
        out_specs=pl.BlockSpec((1, step, HEAD_W), lambda bi, h, t: (bi, t, h)),
        out_shape=jax.ShapeDtypeStruct((b, s, nh * HEAD_W), BF16),
        scratch_shapes=[pltpu.VMEM((HEAD_W, HEAD_W), F32)],
        compiler_params=_params(("parallel", "parallel", "arbitrary")),
        name="retention",
    )(proj, proj, proj, proj, cos, sin, dmat, qd, kd, cdv)


def _mem_attn_kernel(q_ref, kv_ref, o_ref):
    scale = HEAD_W ** -0.5
    width = MEM_HEADS * HEAD_W
    for h in range(MEM_HEADS):
        q = q_ref[0, :, h * HEAD_W:(h + 1) * HEAD_W]
        mk = kv_ref[0, :, h * HEAD_W:(h + 1) * HEAD_W]
        mv = kv_ref[0, :, width + h * HEAD_W:width + (h + 1) * HEAD_W]
        logits = lax.dot_general(q, mk, (((1,), (1,)), ((), ())), preferred_element_type=F32) * scale
        mx = jnp.max(logits, axis=-1, keepdims=True)
        p = jnp.exp(logits - mx)
        p = p / jnp.sum(p, axis=-1, keepdims=True)
        o = jnp.dot(p.astype(BF16), mv, preferred_element_type=F32)
        o_ref[0, :, h * HEAD_W:(h + 1) * HEAD_W] = o.astype(o_ref.dtype)


def mem_attention(proj, mkv, col_block, tq=2048):
    b, s, _ = proj.shape
    mt = mkv.shape[1]
    width = MEM_HEADS * HEAD_W
    tq = _pick(s, tq)
    return pl.pallas_call(
        _mem_attn_kernel,
        grid=(b, s // tq),
        in_specs=[pl.BlockSpec((1, tq, width), lambda bi, t: (bi, t, col_block)),
                  pl.BlockSpec((1, mt, 2 * width), lambda bi, t: (bi, 0, 0))],
        out_specs=pl.BlockSpec((1, tq, width), lambda bi, t: (bi, t, 0)),
        out_shape=jax.ShapeDtypeStruct((b, s, width), BF16),
        compiler_params=_params(("parallel", "parallel")),
        name="mem_attention",
    )(proj, mkv)


DIFF_TILE = 512
LOG2E = math.log2(math.e)


def _t5_bucket(rel):
    half = NUM_BUCKETS // 2
    max_exact = half // 2
    ret = jnp.where(rel > 0, half, 0)
    n = jnp.abs(rel)
    nf = jnp.maximum(n, 1).astype(F32)
    large = max_exact + (jnp.log(nf / max_exact) / math.log(MAX_DISTANCE / max_exact)
                         * (half - max_exact)).astype(jnp.int32)
    large = jnp.minimum(large, half - 1)
    return ret + jnp.where(n < max_exact, n, large)


def _bias_buckets(t):
    i = jnp.arange(t)
    rel = i[:, None] - i[None, :]
    bucket = _t5_bucket(jnp.stack([rel, rel - t]))
    masked = (i[:, None] // CHUNK) > (i[None, :] // CHUNK)
    return bucket.at[0].set(jnp.where(masked, NUM_BUCKETS, bucket[0]))


def _bias_kernel(table_ref, bucket_ref, o_ref):
    h = pl.program_id(0)
    far = table_ref[NUM_BUCKETS // 2 - 1, h]
    rows = 64

    def body(c, carry):
        for slot in range(2):
            b = bucket_ref[slot, pl.ds(c * rows, rows), :]
            acc = jnp.full(b.shape, -jnp.inf, F32)
            for i in range(NUM_BUCKETS):
                acc = jnp.where(b == i, (table_ref[i, h] - far) * LOG2E, acc)
            o_ref[0, slot, pl.ds(c * rows, rows), :] = acc
        return carry

    lax.fori_loop(0, bucket_ref.shape[1] // rows, body, 0)


def bias_tiles(rel_bias):
    t = DIFF_TILE
    assert t >= MAX_DISTANCE and t % CHUNK == 0
    nh = rel_bias.shape[1]
    return pl.pallas_call(
        _bias_kernel,
        grid=(nh,),
        in_specs=[pl.BlockSpec(memory_space=pltpu.SMEM),
                  pl.BlockSpec((2, t, t), lambda h: (0, 0, 0))],
        out_specs=pl.BlockSpec((1, 2, t, t), lambda h: (h, 0, 0, 0)),
        out_shape=jax.ShapeDtypeStruct((nh, 2, t, t), F32),
        compiler_params=_params(("parallel",)),
        name="bias_tiles",
    )(rel_bias.astype(F32), _bias_buckets(t))


V_ROWS = HEAD_W + 16
STALE_MAX_SLACK = 64.0


def _diff_attn_kernel(q_ref, k_ref, vt_ref, bias_ref, lq1_ref, lk1_ref, lq2_ref, lk2_ref, subln_ref,
                      o_ref, acc_ref, *, lam_init):
    t = DIFF_TILE
    i = pl.program_id(2)
    qt = (q_ref[0].astype(F32) * (HALF_W ** -0.5 * LOG2E)).T
    qts = (qt[:HALF_W].astype(BF16), qt[HALF_W:].astype(BF16))
    acc_ref[...] = jnp.zeros_like(acc_ref)

    def all_scores(items):
        out = []
        for kb, _ in items:
            kc = k_ref[0, pl.ds(kb * t, t), :]
            out.append([jnp.dot(kc[:, half * HALF_W:(half + 1) * HALF_W], qts[half],
                                preferred_element_type=F32) for half in range(2)])
        return out

    def process(items, m):
        m = list(m)
        for (kb, slot), s_pair in zip(items, all_scores(items)):
            vt = vt_ref[0, 0, kb]
            for half in range(2):
                s = s_pair[half]
                if slot is not None:
                    s = s + bias_ref[0, slot]
                m_new = jnp.maximum(m[half], jnp.max(s, axis=0, keepdims=True))
                p = jnp.exp2(s - m_new)
                pv = jnp.dot(vt, p.astype(BF16), preferred_element_type=F32)
                acc_ref[half] = jnp.exp2(m[half] - m_new) * acc_ref[half] + pv
                m[half] = m_new
        return tuple(m)

    def process_stale(items, m):
        pvs, cms = [None, None], [None, None]
        for (kb, slot), s_pair in zip(items, all_scores(items)):
            vt = vt_ref[0, 0, kb]
            for half in range(2):
                s = s_pair[half]
                if slot is not None:
                    s = s + bias_ref[0, slot]
                cm = jnp.max(s, axis=0, keepdims=True)
                pv = jnp.dot(vt, jnp.exp2(s - m[half]).astype(BF16), preferred_element_type=F32)
                cms[half] = cm if cms[half] is None else jnp.maximum(cms[half], cm)
                pvs[half] = pv if pvs[half] is None else pvs[half] + pv
        excess = jnp.maximum(jnp.max(cms[0] - m[0]), jnp.max(cms[1] - m[1]))

        def commit(ops):
            m_old, cms_, pvs_ = ops
            out = []
            for half in range(2):
                m_new = jnp.maximum(m_old[half], cms_[half])
                acc_ref[half] = (acc_ref[half] + pvs_[half]) * jnp.exp2(m_old[half] - m_new)
                out.append(m_new)
            return tuple(out)

        def redo(ops):
            kb0, slot0 = items[0]
            return lax.fori_loop(
                0, len(items),
                lambda n, mm: process([(kb0 + n, None if slot0 is None else slot0 - n)], mm), ops[0])

        return lax.cond(jnp.logical_not(excess <= STALE_MAX_SLACK), redo, commit,
                        (tuple(m), tuple(cms), tuple(pvs)))

    neg = jnp.full((1, t), -jnp.inf, F32)
    n_far = jnp.maximum(i - 1, 0)
    m = lax.cond(n_far >= 1, lambda c: process([(0, None)], c), lambda c: c, (neg, neg))
    rest = jnp.maximum(n_far - 1, 0)
    one, two = rest % 2, (rest // 2) % 2
    m = lax.cond(one == 1, lambda c: process_stale([(1, None)], c), lambda c: c, m)
    m = lax.cond(two == 1, lambda c: process_stale([(1 + one, None), (2 + one, None)], c), lambda c: c, m)
    base = 1 + one + 2 * two
    m = lax.fori_loop(
        0, rest // 4, lambda j, c: process_stale([(base + 4 * j + r, None) for r in range(4)], c), m)
    lax.cond(i >= 1, lambda c: process_stale([(i - 1, 1), (i, 0)], c), lambda c: process([(0, 0)], c), m)

    lam = (jnp.exp(jnp.sum(lq1_ref[...] * lk1_ref[...])) - jnp.exp(jnp.sum(lq2_ref[...] * lk2_ref[...]))
           + lam_init)
    l1 = acc_ref[0, HEAD_W:HEAD_W + 1, :]
    l2 = acc_ref[1, HEAD_W:HEAD_W + 1, :]
    o = acc_ref[0, :HEAD_W, :] * (1.0 / l1) - acc_ref[1, :HEAD_W, :] * (lam / l2)
    o = o * lax.rsqrt(jnp.mean(o * o, axis=0, keepdims=True) + EPS) * subln_ref[...]
    o_ref[0] = (o * (1.0 - lam_init)).T.astype(o_ref.dtype)


def _value_proj_t_kernel(a_ref, w_ref, ssq_ref, o_ref, *, norm_dim):
    acc = jnp.dot(a_ref[...], w_ref[...].astype(BF16), preferred_element_type=F32)
    acc = acc * _row_rstd(ssq_ref[...], norm_dim)
    _, heads, chunks, _, t = o_ref.shape
    ones = jnp.ones((V_ROWS - HEAD_W, t), o_ref.dtype)
    for hh in range(heads):
        for c in range(chunks):
            blk = acc[c * t:(c + 1) * t, hh * HEAD_W:(hh + 1) * HEAD_W]
            o_ref[0, hh, c, :HEAD_W, :] = blk.T.astype(o_ref.dtype)
            o_ref[0, hh, c, HEAD_W:, :] = ones


def value_proj_t(h, ssq, w, col0, batch, seq, bm=2048, bn=512):
    m, k = h.shape
    t = DIFF_TILE
    nh = (w.shape[1] - col0) // HEAD_W
    bm = _pick(seq, bm)
    assert bm % t == 0 and bn % HEAD_W == 0 and col0 % bn == 0 and nh * HEAD_W % bn == 0
    tiles_per_seq = seq // bm
    return pl.pallas_call(
        functools.partial(_value_proj_t_kernel, norm_dim=k),
        grid=(m // bm, nh * HEAD_W // bn),
        in_specs=[pl.BlockSpec((bm, k), lambda i, j: (i, 0), pipeline_mode=pl.Buffered(1)),
                  pl.BlockSpec((k, bn), lambda i, j: (0, col0 // bn + j)),
                  pl.BlockSpec((bm, LANES), lambda i, j: (i, 0))],
        out_specs=pl.BlockSpec((1, bn // HEAD_W, bm // t, V_ROWS, t),
                               lambda i, j: (i // tiles_per_seq, j, i % tiles_per_seq, 0, 0)),
        out_shape=jax.ShapeDtypeStruct((batch, nh, seq // t, V_ROWS, t), BF16),
        compiler_params=_params(("parallel", "arbitrary")),
        name="value_proj_t",
    )(h, w, ssq)


def diff_attention(qproj, kproj, vt, bias, lq1, lk1, lq2, lk2, subln_g, lam_init):
    b, s, _ = qproj.shape
    nh = DIFF_HEADS
    t = DIFF_TILE
    assert s % t == 0
    nc = s // t
    vec = lambda a: a.astype(F32).reshape(1, -1)
    small = lambda n: pl.BlockSpec((1, n), lambda bi, h, qb: (0, 0))
    return pl.pallas_call(
        functools.partial(_diff_attn_kernel, lam_init=lam_init),
        grid=(b, nh, nc),
        in_specs=[pl.BlockSpec((1, t, HEAD_W), lambda bi, h, qb: (bi, qb, h)),
                  pl.BlockSpec((1, s, HEAD_W), lambda bi, h, qb: (bi, 0, h)),
                  pl.BlockSpec((1, 1, nc, V_ROWS, t), lambda bi, h, qb: (bi, h, 0, 0, 0)),
                  pl.BlockSpec((1, 2, t, t), lambda bi, h, qb: (h, 0, 0, 0)),
                  small(HALF_W), small(HALF_W), small(HALF_W), small(HALF_W),
                  pl.BlockSpec((HEAD_W, 1), lambda bi, h, qb: (0, 0))],
        out_specs=pl.BlockSpec((1, t, HEAD_W), lambda bi, h, qb: (bi, qb, h)),
        out_shape=jax.ShapeDtypeStruct((b, s, nh * HEAD_W), BF16),
        scratch_shapes=[pltpu.VMEM((2, V_ROWS, t), F32)],
        compiler_params=_params(("parallel", "parallel", "arbitrary")),
        name="diff_attention",
    )(qproj, kproj, vt, bias, vec(lq1), vec(lk1), vec(lq2), vec(lk2), subln_g.astype(F32).reshape(-1, 1))


def _up_conv_gate_kernel(h_ref, ssq_ref, wa_ref, wg_ref, cwa_ref, cwg_ref, cba_ref, cbg_ref, o_ref,
                         halo_a, halo_g, *, tiles_per_seq, chunks):
    first = (pl.program_id(1) % tiles_per_seq) == 0
    d = h_ref.shape[1]
    wa, wg = wa_ref[...], wg_ref[...]
    rstd = _row_rstd(ssq_ref[...], d)

    us, r0 = [], 0
    for rc in chunks:
        hc = h_ref[r0:r0 + rc, :]
        rs = rstd[r0:r0 + rc]
        us.append((jnp.dot(hc, wa, preferred_element_type=F32) * rs, jnp.dot(hc, wg, preferred_element_type=F32) * rs))
        r0 += rc

    row8 = lax.broadcasted_iota(jnp.int32, (8, wa.shape[1]), 0)

    def conv(u, prev, cw_ref, cb_ref):
        top = u[:8]
        t1 = jnp.where(row8 == 0, prev[7:8], pltpu.roll(top, 1, 0))
        t2 = jnp.where(row8 == 0, prev[6:7], jnp.where(row8 == 1, prev[7:8], pltpu.roll(top, 2, 0)))
        u1 = jnp.concatenate([t1, pltpu.roll(u, 1, 0)[8:]], axis=0)
        u2 = jnp.concatenate([t2, pltpu.roll(u, 2, 0)[8:]], axis=0)
        cw = cw_ref[...]
        return cb_ref[...] + cw[0:1] * u2 + cw[1:2] * u1 + cw[2:3] * u

    prev_a = jnp.where(first, 0.0, halo_a[...])
    prev_g = jnp.where(first, 0.0, halo_g[...])
    r0 = 0
    for rc, (ua, ug) in zip(chunks, us):
        a = conv(ua, prev_a, cwa_ref, cba_ref)
        g = conv(ug, prev_g, cwg_ref, cbg_ref)
        o_ref[r0:r0 + rc, :] = (a * jax.nn.sigmoid(a) * g).astype(o_ref.dtype)
        prev_a, prev_g = ua[rc - 8:], ug[rc - 8:]
        r0 += rc
    halo_a[...] = prev_a
    halo_g[...] = prev_g


def _row_chunks(bm, rc):
    rc = _pick(bm, rc)
    return (rc,) * (bm // rc)


def up_conv_gate(h, ssq, w_up, conv_w, conv_b, layer, seq, bm=1024, bn=512, rc=256):
    m, d = h.shape
    f = w_up.shape[2] // 2
    bm, bn = _pick(seq, bm), _pick(f, bn)
    nj = f // bn

    def cols(rows_, off):
        return pl.BlockSpec((None, rows_, bn), lambda j, i: (layer, 0, off + j))

    return pl.pallas_call(
        functools.partial(_up_conv_gate_kernel, tiles_per_seq=seq // bm, chunks=_row_chunks(bm, rc)),
        grid=(nj, m // bm),
        in_specs=[pl.BlockSpec((bm, d), lambda j, i: (i, 0)),
                  pl.BlockSpec((bm, LANES), lambda j, i: (i, 0)),
                  cols(d, 0), cols(d, nj), cols(3, 0), cols(3, nj), cols(1, 0), cols(1, nj)],
        out_specs=pl.BlockSpec((bm, bn), lambda j, i: (i, j)),
        out_shape=jax.ShapeDtypeStruct((m, f), BF16),
        scratch_shapes=[pltpu.VMEM((8, bn), F32), pltpu.VMEM((8, bn), F32)],
        compiler_params=_params(("parallel", "arbitrary")),
        name="up_conv_gate",
    )(h, ssq, w_up, w_up, conv_w, conv_w, conv_b, conv_b)


def kernel(x, mem, a_w_in, b_w_q, w_kv, w_out, mem_w_kv, ffn_w_up, ffn_conv_w, ffn_conv_b, ffn_w_down,
           attn_norm, ffn_norm, kv_norm, mem_norm, final_norm, rel_bias,
           lambda_q1, lambda_k1, lambda_q2, lambda_k2, diff_subln):
    b, s, d = x.shape
    depth = w_out.shape[0]
    n_a = a_w_in.shape[0]
    m = b * s
    mix_w = RET_HEADS * HEAD_W
    mem_w = MEM_HEADS * HEAD_W
    assert d == mix_w + mem_w and a_w_in.shape[2] == 4 * mix_w + mem_w and b_w_q.shape[2] == mix_w + mem_w

    pos = jnp.arange(s, dtype=F32)
    inv_freq = ROPE_BASE ** (-jnp.arange(HALF_W, dtype=F32) / HALF_W)
    ang = pos[:, None] * inv_freq[None, :]
    cos, sin = jnp.cos(ang), jnp.sin(ang)

    xf = x.reshape(m, d)
    memn = rmsnorm(mem.reshape(-1, d), mem_norm, BF16)
    w_up = ffn_w_up.astype(BF16)
    w_down = ffn_w_down.astype(BF16)
    conv_b = ffn_conv_b[:, None, :]
    h, ssq = rmsnorm(xf, attn_norm[0], BF16), None
    hkv = kproj = vt = bias = None
    for layer in range(depth):
        if layer == n_a:
            kproj = matmul([hkv], w_kv[None], 0, BF16, ssq=ssq, n_cols=mix_w, name="key_proj").reshape(b, s, -1)
            vt = value_proj_t(hkv, ssq, w_kv, mix_w, b, s)
            bias = bias_tiles(rel_bias)
        mkv = matmul([memn], mem_w_kv, layer, BF16).reshape(b, -1, 2 * mem_w)
        if layer < n_a:
            proj = matmul([h], a_w_in, layer, BF16, ssq=ssq).reshape(b, s, -1)
            mix = retention(proj, cos, sin)
            mo = mem_attention(proj, mkv, (4 * mix_w) // mem_w)
        else:
            bl = layer - n_a
            lam_init = 0.8 - 0.6 * math.exp(-0.3 * layer)
            proj = matmul([h], b_w_q, bl, BF16, ssq=ssq).reshape(b, s, -1)
            mix = diff_attention(proj, kproj, vt, bias, lambda_q1[bl], lambda_k1[bl], lambda_q2[bl],
                                 lambda_k2[bl], diff_subln[bl], lam_init)
            mo = mem_attention(proj, mkv, mix_w // mem_w)
        xf, (h,), ssq = matmul([mix.reshape(m, mix_w), mo.reshape(m, mem_w)], w_out, layer, F32, resid=xf,
                               gains=[ffn_norm[layer]], bm=2048, bn=256, name="out_proj")
        act = up_conv_gate(h, ssq, w_up, ffn_conv_w, conv_b, layer, s)
        if layer + 1 == depth:
            xf = matmul([act], w_down, layer, F32, resid=xf, bm=1024, bn=512, name="down_proj")
        else:
            gains = [attn_norm[layer + 1]] + ([kv_norm] if layer + 1 == n_a else [])
            xf, hs, ssq = matmul([act], w_down, layer, F32, resid=xf, gains=gains,
                                 bm=1024, bn=512, name="down_proj")
            h, hkv = hs[0], hs[-1]
    return rmsnorm(xf, final_norm, F32).reshape(b, s, d)
```
